```python
import jax, jax.numpy as jnp
from jax import lax
import numpy as np

D_MODEL = 1024
BATCH = 8
SEQ = 2048
DEPTH = 2

HEAD_DIM = 64
A_GROUPS = 4
A_CHUNK = 128
A_GROUP_DIM = 128
A_WIDTH = A_GROUPS * A_GROUP_DIM
B_PATTERNS = ((128, 1), (512, 4), (2048, 16))
B_GROUPS = len(B_PATTERNS)
B_HEADS_PER_GROUP = 4
B_WIDTH = B_GROUPS * B_HEADS_PER_GROUP * HEAD_DIM
B_OUT = B_HEADS_PER_GROUP * HEAD_DIM
B_QBLOCK = 128
C_HEADS = 8
C_WIDTH = C_HEADS * HEAD_DIM
C_QBLOCK = 128
N_BRANCH = 3
SPLIT_A = 2 * A_WIDTH
SPLIT_B = SPLIT_A + 3 * B_WIDTH
SPLIT_C = SPLIT_B + 3 * C_WIDTH
IN_COLS = SPLIT_C + N_BRANCH * D_MODEL
D_FF = 2816
EPS = 1e-6

kernel_name = "hybrid_gated_gmlp_dilated_stickbreaking_block"


def rms_norm(x, g):
    xf = x.astype(jnp.float32)
    y = xf * lax.rsqrt(jnp.mean(xf * xf, axis=-1, keepdims=True) + EPS)
    return (y * g.astype(jnp.float32)).astype(x.dtype)


def layer_norm(x, g, b):
    xf = x.astype(jnp.float32)
    mu = jnp.mean(xf, axis=-1, keepdims=True)
    xc = xf - mu
    y = xc * lax.rsqrt(jnp.mean(xc * xc, axis=-1, keepdims=True) + EPS)
    return (y * g.astype(jnp.float32) + b.astype(jnp.float32)).astype(x.dtype)


def swiglu(h, wi, wo):
    gate, up = jnp.split(h @ wi, 2, axis=-1)
    return (jax.nn.silu(gate) * up) @ wo


def chunked_spatial_gating(z, ln_g, ln_b, w_s, b_s):
    u, v = jnp.split(z, 2, axis=-1)
    v = layer_norm(v, ln_g, ln_b)
    bsz, s = v.shape[0], v.shape[1]
    n_chunks = s // A_CHUNK
    v = v.reshape(bsz, n_chunks, A_CHUNK, A_GROUPS, A_GROUP_DIM)
    causal = jnp.tril(jnp.ones((A_CHUNK, A_CHUNK), dtype=bool))
    w = jnp.where(causal, w_s, 0)
    sv = jnp.einsum('gts,bcsgd->bctgd', w, v) + b_s.T[:, :, None]
    return u * sv.reshape(bsz, s, A_WIDTH)


def dilated_attention(q, k, v):
    bsz, s = q.shape[0], q.shape[1]
    n_blocks = s // B_QBLOCK
    scale = HEAD_DIM ** -0.5
    q_blocks = q.reshape(bsz, n_blocks, B_QBLOCK, B_GROUPS, B_HEADS_PER_GROUP, HEAD_DIM).swapaxes(0, 1)
    starts = jnp.arange(n_blocks) * B_QBLOCK

    def block(args):
        q_blk, start = args
        t = start + jnp.arange(B_QBLOCK)
        outs, maxes, sums = [], [], []
        for g, (window, dil) in enumerate(B_PATTERNS):
            n_keys = window // dil + 1
            idx = t[:, None] - dil * jnp.arange(n_keys)[None, :]
            valid = idx >= 0
            idx = jnp.maximum(idx, 0)
            kg = k[:, :, g][:, idx]
            vg = v[:, :, g][:, idx]
            sc = jnp.einsum('bthd,btjhd->bhtj', q_blk[:, :, g], kg).astype(jnp.float32) * scale
            sc = jnp.where(valid[None, None], sc, -jnp.inf)
            m = jnp.max(sc, axis=-1, keepdims=True)
            p = jnp.exp(sc - m)
            l = jnp.sum(p, axis=-1, keepdims=True)
            o = jnp.einsum('bhtj,btjhd->bhtd', p, vg.astype(jnp.float32)) / l
            outs.append(o)
            maxes.append(m)
            sums.append(l)
        m_all = jnp.stack(maxes)
        wts = jnp.stack(sums) * jnp.exp(m_all - jnp.max(m_all, axis=0, keepdims=True))
        out = jnp.sum(wts * jnp.stack(outs), axis=0) / jnp.sum(wts, axis=0)
        return out.transpose(0, 2, 1, 3).astype(q.dtype)

    out = lax.map(block, (q_blocks, starts))
    return out.swapaxes(0, 1).reshape(bsz, s, B_OUT)


def stick_breaking_attention(q, k, v):
    bsz, s = q.shape[0], q.shape[1]
    n_blocks = s // C_QBLOCK
    scale = HEAD_DIM ** -0.5
    q_blocks = q.reshape(bsz, n_blocks, C_QBLOCK, C_HEADS, HEAD_DIM).swapaxes(0, 1)
    starts = jnp.arange(n_blocks) * C_QBLOCK
    kpos = jnp.arange(s)
    v32 = v.astype(jnp.float32)

    def block(args):
        q_blk, start = args
        t = start + jnp.arange(C_QBLOCK)
        z = jnp.einsum('bthd,bshd->bhts', q_blk, k).astype(jnp.float32) * scale
        causal = kpos[None, :] < t[:, None]
        log_one_minus = jnp.where(causal, jax.nn.log_sigmoid(-z), 0.0)
        after = lax.cumsum(log_one_minus, axis=3, reverse=True) - log_one_minus
        a = jnp.where(causal, jnp.exp(jax.nn.log_sigmoid(z) + after), 0.0)
        o = jnp.einsum('bhts,bshd->bthd', a, v32)
        return o.astype(q.dtype)

    out = lax.map(block, (q_blocks, starts))
    return out.swapaxes(0, 1).reshape(bsz, s, C_WIDTH)


def hybrid_mixer(h, w_in, a_ln_g, a_ln_b, a_ws, a_bs, w_pa, w_pb, w_pc, w_o):
    bsz, s = h.shape[0], h.shape[1]
    proj = h @ w_in
    za, qkv_b, qkv_c, gate_pre = jnp.split(proj, [SPLIT_A, SPLIT_B, SPLIT_C], axis=-1)
    y_a = chunked_spatial_gating(jax.nn.gelu(za), a_ln_g, a_ln_b, a_ws, a_bs)
    qb, kb, vb = [t.reshape(bsz, s, B_GROUPS, B_HEADS_PER_GROUP, HEAD_DIM)
                  for t in jnp.split(qkv_b, 3, axis=-1)]
    y_b = dilated_attention(qb, kb, vb)
    qc, kc, vc = [t.reshape(bsz, s, C_HEADS, HEAD_DIM) for t in jnp.split(qkv_c, 3, axis=-1)]
    y_c = stick_breaking_attention(qc, kc, vc)
    g_a, g_b, g_c = jnp.split(jax.nn.sigmoid(gate_pre), 3, axis=-1)
    merged = g_a * (y_a @ w_pa) + g_b * (y_b @ w_pb) + g_c * (y_c @ w_pc)
    return merged @ w_o


def setup_inputs(seed: int = 0) -> dict:
    key = jax.random.key(seed)
    ks = jax.random.split(key, 24)
    f32 = jnp.float32

    def dense(k, fan_in, fan_out):
        return jax.random.normal(k, (DEPTH, fan_in, fan_out), f32) * fan_in ** -0.5

    def gain(k, n):
        return 1.0 + 0.01 * jax.random.normal(k, (DEPTH, n), f32)

    return {
        "x": jax.random.normal(ks[0], (BATCH, SEQ, D_MODEL), f32),
        "ffn1_pre_g": gain(ks[1], D_MODEL),
        "ffn1_wi": dense(ks[2], D_MODEL, 2 * D_FF),
        "ffn1_wo": dense(ks[3], D_FF, D_MODEL),
        "ffn1_post_g": gain(ks[4], D_MODEL),
        "mix_pre_g": gain(ks[5], D_MODEL),
        "w_in": dense(ks[6], D_MODEL, IN_COLS),
        "a_ln_g": gain(ks[7], A_WIDTH),
        "a_ln_b": 0.01 * jax.random.normal(ks[8], (DEPTH, A_WIDTH), f32),
        "a_ws": jax.random.normal(ks[9], (DEPTH, A_GROUPS, A_CHUNK, A_CHUNK), f32) * A_CHUNK ** -0.5,
        "a_bs": 1.0 + 0.01 * jax.random.normal(ks[10], (DEPTH, A_GROUPS, A_CHUNK), f32),
        "w_pa": dense(ks[11], A_WIDTH, D_MODEL),
        "w_pb": dense(ks[12], B_OUT, D_MODEL),
        "w_pc": dense(ks[13], C_WIDTH, D_MODEL),
        "w_o": dense(ks[14], D_MODEL, D_MODEL),
        "mix_post_g": gain(ks[15], D_MODEL),
        "ffn2_pre_g": gain(ks[16], D_MODEL),
        "ffn2_wi": dense(ks[17], D_MODEL, 2 * D_FF),
        "ffn2_wo": dense(ks[18], D_FF, D_MODEL),
        "ffn2_post_g": gain(ks[19], D_MODEL),
    }


def reference(x, ffn1_pre_g, ffn1_wi, ffn1_wo, ffn1_post_g, mix_pre_g, w_in, a_ln_g, a_ln_b,
              a_ws, a_bs, w_pa, w_pb, w_pc, w_o, mix_post_g, ffn2_pre_g, ffn2_wi, ffn2_wo,
              ffn2_post_g):
    for l in range(DEPTH):
        h = rms_norm(x, ffn1_pre_g[l])
        x = x + 0.5 * rms_norm(swiglu(h, ffn1_wi[l], ffn1_wo[l]), ffn1_post_g[l])
        h = rms_norm(x, mix_pre_g[l])
        y = hybrid_mixer(h, w_in[l], a_ln_g[l], a_ln_b[l], a_ws[l], a_bs[l],
                         w_pa[l], w_pb[l], w_pc[l], w_o[l])
        x = x + rms_norm(y, mix_post_g[l])
        h = rms_norm(x, ffn2_pre_g[l])
        x = x + 0.5 * rms_norm(swiglu(h, ffn2_wi[l], ffn2_wo[l]), ffn2_post_g[l])
    return x
```

```python
import functools

import jax
import jax.numpy as jnp
from jax import lax
from jax.experimental import pallas as pl
from jax.experimental.pallas import tpu as pltpu

D_MODEL = 1024
HEAD_DIM = 64
A_GROUPS = 4
A_CHUNK = 128
A_GROUP_DIM = 128
A_WIDTH = A_GROUPS * A_GROUP_DIM
B_PATTERNS = ((128, 1), (512, 4), (2048, 16))
B_GROUPS = len(B_PATTERNS)
B_HEADS_PER_GROUP = 4
B_GROUP_WIDTH = B_HEADS_PER_GROUP * HEAD_DIM
B_WIDTH = B_GROUPS * B_GROUP_WIDTH
B_BLOCK = 128
C_HEADS = 8
C_WIDTH = C_HEADS * HEAD_DIM
C_BLOCK = 256
D_FF = 2816
EPS = 1e-6
SCALE = HEAD_DIM ** -0.5

F32 = jnp.float32
BF16 = jnp.bfloat16

VMEM_LIMIT_BYTES = 56 * 1024 * 1024

FFN_ROWS = 1024
FFN_COLS = 256
PROJ_ROWS = 256
MERGE_ROWS = 256


def _params(*semantics):
    return pltpu.CompilerParams(dimension_semantics=semantics, vmem_limit_bytes=VMEM_LIMIT_BYTES)


def _rms(x, g):
    return x * lax.rsqrt(jnp.mean(x * x, axis=-1, keepdims=True) + EPS) * g


def _resident(shape):
    return pl.BlockSpec(shape, lambda *_: (0,) * len(shape), pipeline_mode=pl.Buffered(1))


def _ffn_kernel(x_ref, pre_g_ref, wi_ref, wo_ref, post_g_ref, o_ref, h_ref, acc_ref):
    c = pl.program_id(1)

    @pl.when(c == 0)
    def _():
        h_ref[...] = _rms(x_ref[...], pre_g_ref[...]).astype(BF16)
        acc_ref[...] = jnp.zeros_like(acc_ref)

    gate_up = jnp.dot(h_ref[...], wi_ref[...], preferred_element_type=F32)
    gate = gate_up[:, :FFN_COLS]
    up = gate_up[:, FFN_COLS:]
    act = (gate * jax.nn.sigmoid(gate) * up).astype(BF16)
    acc_ref[...] += jnp.dot(act, wo_ref[...], preferred_element_type=F32)

    @pl.when(c == pl.num_programs(1) - 1)
    def _():
        o_ref[...] = x_ref[...] + 0.5 * _rms(acc_ref[...], post_g_ref[...])


def _ffn(x, pre_g, wi_cat, wo, post_g):
    n = x.shape[0]
    grid = (n // FFN_ROWS, D_FF // FFN_COLS)
    return pl.pallas_call(
        _ffn_kernel,
        grid=grid,
        in_specs=[
            pl.BlockSpec((FFN_ROWS, D_MODEL), lambda i, c: (i, 0)),
            pl.BlockSpec((1, D_MODEL), lambda i, c: (0, 0)),
            pl.BlockSpec((D_MODEL, 2 * FFN_COLS), lambda i, c: (0, c)),
            pl.BlockSpec((FFN_COLS, D_MODEL), lambda i, c: (c, 0)),
            pl.BlockSpec((1, D_MODEL), lambda i, c: (0, 0)),
        ],
        out_specs=pl.BlockSpec((FFN_ROWS, D_MODEL), lambda i, c: (i, 0)),
        out_shape=jax.ShapeDtypeStruct((n, D_MODEL), F32),
        scratch_shapes=[pltpu.VMEM((FFN_ROWS, D_MODEL), BF16), pltpu.VMEM((FFN_ROWS, D_MODEL), F32)],
        compiler_params=_params("parallel", "arbitrary"),
        name="ffn",
    )(x, pre_g, wi_cat, wo, post_g)


def _interleave_gate_up(wi):
    d = wi.shape[0]
    nc = D_FF // FFN_COLS
    w = wi.reshape(d, 2, nc, FFN_COLS).transpose(0, 2, 1, 3)
    return w.reshape(d, 2 * D_FF).astype(BF16)


def _proj_kernel(x_ref, g_ref, wa_ref, wb_ref, wc_ref, wg_ref, za_ref, qkvb_ref, qkvc_ref, gate_ref):
    h = _rms(x_ref[...], g_ref[...]).astype(BF16)
    za_ref[...] = jnp.dot(h, wa_ref[...], preferred_element_type=F32)
    qkvb_ref[...] = jnp.dot(h, wb_ref[...], preferred_element_type=F32).astype(BF16)
    qkvc_ref[...] = jnp.dot(h, wc_ref[...], preferred_element_type=F32).astype(BF16)
    gate_ref[...] = jnp.dot(h, wg_ref[...], preferred_element_type=F32)


def _proj(x, g, wa, wb, wc, wg):
    n = x.shape[0]
    rows = lambda w: pl.BlockSpec((PROJ_ROWS, w), lambda i: (i, 0))
    return pl.pallas_call(
        _proj_kernel,
        grid=(n // PROJ_ROWS,),
        in_specs=[rows(D_MODEL), _resident((1, D_MODEL)), _resident(wa.shape), _resident(wb.shape),
                  _resident(wc.shape), _resident(wg.shape)],
        out_specs=[rows(wa.shape[1]), rows(wb.shape[1]), rows(wc.shape[1]), rows(wg.shape[1])],
        out_shape=[jax.ShapeDtypeStruct((n, wa.shape[1]), F32),
                   jax.ShapeDtypeStruct((n, wb.shape[1]), BF16),
                   jax.ShapeDtypeStruct((n, wc.shape[1]), BF16),
                   jax.ShapeDtypeStruct((n, wg.shape[1]), F32)],
        compiler_params=_params("parallel"),
        name="proj",
    )(x, g, wa, wb, wc, wg)


def _band_attention_kernel(q_ref, k_ref, v_ref, o_ref, lse_ref):
    n_blocks = q_ref.shape[0] // B_BLOCK
    iq = lax.broadcasted_iota(jnp.int32, (B_BLOCK, B_BLOCK), 0)
    ik = lax.broadcasted_iota(jnp.int32, (B_BLOCK, B_BLOCK), 1)
    contract_last = (((1,), (1,)), ((), ()))

    def block(i, carry):
        cur = pl.ds(pl.multiple_of(i * B_BLOCK, B_BLOCK), B_BLOCK)
        prev = pl.ds(pl.multiple_of(jnp.maximum(i - 1, 0) * B_BLOCK, B_BLOCK), B_BLOCK)
        prev_ok = jnp.logical_and(ik >= iq, i > 0)
        for h in range(B_HEADS_PER_GROUP):
            cols = slice(h * HEAD_DIM, (h + 1) * HEAD_DIM)
            q = q_ref[cur, cols]
            s_cur = lax.dot_general(q, k_ref[cur, cols], contract_last, preferred_element_type=F32) * SCALE
            s_prev = lax.dot_general(q, k_ref[prev, cols], contract_last, preferred_element_type=F32) * SCALE
            s_cur = jnp.where(ik <= iq, s_cur, -jnp.inf)
            s_prev = jnp.where(prev_ok, s_prev, -jnp.inf)
            m = jnp.maximum(jnp.max(s_cur, axis=1, keepdims=True), jnp.max(s_prev, axis=1, keepdims=True))
            p_cur = jnp.exp(s_cur - m)
            p_prev = jnp.exp(s_prev - m)
            l = jnp.sum(p_cur, axis=1, keepdims=True) + jnp.sum(p_prev, axis=1, keepdims=True)
            o = jnp.dot(p_cur.astype(BF16), v_ref[cur, cols], preferred_element_type=F32)
            o += jnp.dot(p_prev.astype(BF16), v_ref[prev, cols], preferred_element_type=F32)
            o_ref[cur, cols] = o / l
            lse_ref[cur, cols] = jnp.broadcast_to(m + jnp.log(l), (B_BLOCK, HEAD_DIM))
        return carry

    lax.fori_loop(0, n_blocks, block, 0)


def _band_attention(qkvb, group):
    bsz, s, width = qkvb.shape
    dil = B_PATTERNS[group][1]
    length = s // dil
    col_blocks = width // B_GROUP_WIDTH
    view = qkvb.reshape(bsz, length, dil * width)

    def spec(part):
        offset = part * B_GROUPS + group
        return pl.BlockSpec((None, length, B_GROUP_WIDTH), lambda b, r: (b, 0, r * col_blocks + offset))

    out_spec = pl.BlockSpec((None, length, B_GROUP_WIDTH), lambda b, r: (b, 0, r))
    out_shape = jax.ShapeDtypeStruct((bsz, length, dil * B_GROUP_WIDTH), F32)
    o, lse = pl.pallas_call(
        _band_attention_kernel,
        grid=(bsz, dil),
        in_specs=[spec(0), spec(1), spec(2)],
        out_specs=[out_spec, out_spec],
        out_shape=[out_shape, out_shape],
        compiler_params=_params("parallel", "parallel"),
        name=f"band_attention_g{group}",
    )(view, view, view)
    return o.reshape(bsz, s, B_GROUP_WIDTH), lse.reshape(bsz, s, B_GROUP_WIDTH)


def _softplus(z):
    return jnp.maximum(z, 0.0) + jnp.log(1.0 + jnp.exp(-jnp.abs(z)))


def _suffix_sums(x, upper):
    hi = x.astype(BF16)
    lo = (x - hi.astype(F32)).astype(BF16)
    return (jnp.dot(hi, upper, preferred_element_type=F32) + jnp.dot(lo, upper, preferred_element_type=F32))


def _stick_breaking_kernel(q_ref, k_ref, v_ref, o_ref):
    n_blocks = q_ref.shape[0] // C_BLOCK
    row = lax.broadcasted_iota(jnp.int32, (C_BLOCK, C_BLOCK), 0)
    col = lax.broadcasted_iota(jnp.int32, (C_BLOCK, C_BLOCK), 1)
    strictly_lower = col < row
    upper = jnp.where(row > col, 1.0, 0.0).astype(BF16)
    contract_last = (((1,), (1,)), ((), ()))

    for h in range(q_ref.shape[1] // HEAD_DIM):
        cols = slice(h * HEAD_DIM, (h + 1) * HEAD_DIM)

        def q_block(i, carry, cols=cols):
            rows_i = pl.ds(pl.multiple_of(i * C_BLOCK, C_BLOCK), C_BLOCK)
            q = q_ref[rows_i, cols]

            z = lax.dot_general(q, k_ref[rows_i, cols], contract_last, preferred_element_type=F32) * SCALE
            log_keep = jnp.where(strictly_lower, -_softplus(z), 0.0)
            after = _suffix_sums(log_keep, upper)
            a = jnp.where(strictly_lower, jnp.exp(log_keep + z + after), 0.0)
            acc = jnp.dot(a.astype(BF16), v_ref[rows_i, cols], preferred_element_type=F32)
            passed = jnp.sum(log_keep, axis=1, keepdims=True)

            def k_block(step, state):
                acc, passed = state
                rows_j = pl.ds(pl.multiple_of((i - 1 - step) * C_BLOCK, C_BLOCK), C_BLOCK)
                z = lax.dot_general(q, k_ref[rows_j, cols], contract_last, preferred_element_type=F32) * SCALE
                log_keep = -_softplus(z)
                after = _suffix_sums(log_keep, upper) + passed
                a = jnp.exp(log_keep + z + after)
                acc = acc + jnp.dot(a.astype(BF16), v_ref[rows_j, cols], preferred_element_type=F32)
                return acc, passed + jnp.sum(log_keep, axis=1, keepdims=True)

            acc, _ = lax.fori_loop(0, i, k_block, (acc, passed))
            o_ref[rows_i, cols] = acc.astype(o_ref.dtype)
            return carry

        lax.fori_loop(0, n_blocks, q_block, 0)


def _stick_breaking(qkvc):
    bsz, s, _ = qkvc.shape
    pair = 2 * HEAD_DIM
    n_pairs = C_WIDTH // pair

    def spec(part):
        return pl.BlockSpec((None, s, pair), lambda b, p: (b, 0, part * n_pairs + p))

    return pl.pallas_call(
        _stick_breaking_kernel,
        grid=(bsz, n_pairs),
        in_specs=[spec(0), spec(1), spec(2)],
        out_specs=pl.BlockSpec((None, s, pair), lambda b, p: (b, 0, p)),
        out_shape=jax.ShapeDtypeStruct((bsz, s, C_WIDTH), BF16),
        compiler_params=_params("parallel", "parallel"),
        name="stick_breaking",
    )(qkvc, qkvc, qkvc)


def _merge_kernel(x_ref, za_ref, gate_ref, o0_ref, l0_ref, o1_ref, l1_ref, o2_ref, l2_ref, yc_ref,
                  ln_g_ref, ln_b_ref, ws_ref, bs_ref, wpa_ref, wpb_ref, wpc_ref, wo_ref, post_g_ref,
                  out_ref):
    z = jax.nn.gelu(za_ref[...], approximate=True)
    u = z[:, :A_WIDTH]
    v = z[:, A_WIDTH:]
    mu = jnp.mean(v, axis=-1, keepdims=True)
    vc = v - mu
    v = vc * lax.rsqrt(jnp.mean(vc * vc, axis=-1, keepdims=True) + EPS) * ln_g_ref[...] + ln_b_ref[...]
    v = v.astype(BF16)
    t_idx = lax.broadcasted_iota(jnp.int32, (A_CHUNK, A_CHUNK), 0)
    s_idx = lax.broadcasted_iota(jnp.int32, (A_CHUNK, A_CHUNK), 1)
    chunks = []
    for c in range(MERGE_ROWS // A_CHUNK):
        groups = []
        for g in range(A_GROUPS):
            w = jnp.where(s_idx <= t_idx, ws_ref[g], 0.0).astype(BF16)
            vg = v[c * A_CHUNK:(c + 1) * A_CHUNK, g * A_GROUP_DIM:(g + 1) * A_GROUP_DIM]
            groups.append(jnp.dot(w, vg, preferred_element_type=F32))
        chunks.append(jnp.concatenate(groups, axis=1) + bs_ref[...])
    y_a = u * jnp.concatenate(chunks, axis=0)

    l0, l1, l2 = l0_ref[...], l1_ref[...], l2_ref[...]
    m = jnp.maximum(jnp.maximum(l0, l1), l2)
    w0, w1, w2 = jnp.exp(l0 - m), jnp.exp(l1 - m), jnp.exp(l2 - m)
    y_b = (w0 * o0_ref[...] + w1 * o1_ref[...] + w2 * o2_ref[...]) / (w0 + w1 + w2)

    gates = jax.nn.sigmoid(gate_ref[...])
    merged = gates[:, :D_MODEL] * jnp.dot(y_a.astype(BF16), wpa_ref[...], preferred_element_type=F32)
    merged += gates[:, D_MODEL:2 * D_MODEL] * jnp.dot(y_b.astype(BF16), wpb_ref[...], preferred_element_type=F32)
    merged += gates[:, 2 * D_MODEL:] * jnp.dot(yc_ref[...], wpc_ref[...], preferred_element_type=F32)
    y = jnp.dot(merged.astype(BF16), wo_ref[...], preferred_element_type=F32)
    out_ref[...] = x_ref[...] + _rms(y, post_g_ref[...])


def _merge(x, za, gate, b_parts, yc, ln_g, ln_b, ws, bs_full, wpa, wpb, wpc, wo, post_g):
    n = x.shape[0]
    rows = lambda w: pl.BlockSpec((MERGE_ROWS, w), lambda i: (i, 0))
    b_flat = [a for part in b_parts for a in part]
    return pl.pallas_call(
        _merge_kernel,
        grid=(n // MERGE_ROWS,),
        in_specs=[rows(D_MODEL), rows(2 * A_WIDTH), rows(3 * D_MODEL)]
        + [rows(B_GROUP_WIDTH)] * 6
        + [rows(C_WIDTH), _resident(ln_g.shape), _resident(ln_b.shape), _resident(ws.shape),
           _resident(bs_full.shape), _resident(wpa.shape), _resident(wpb.shape), _resident(wpc.shape),
           _resident(wo.shape), _resident(post_g.shape)],
        out_specs=rows(D_MODEL),
        out_shape=jax.ShapeDtypeStruct((n, D_MODEL), F32),
        compiler_params=_params("parallel"),
        name="merge",
    )(x, za, gate, *b_flat, yc, ln_g, ln_b, ws, bs_full, wpa, wpb, wpc, wo, post_g)


def kernel(x, ffn1_pre_g, ffn1_wi, ffn1_wo, ffn1_post_g, mix_pre_g, w_in, a_ln_g, a_ln_b, a_ws, a_bs,
           w_pa, w_pb, w_pc, w_o, mix_post_g, ffn2_pre_g, ffn2_wi, ffn2_wo, ffn2_post_g):
    bsz, s, d = x.shape
    n = bsz * s
    depth = ffn1_wi.shape[0]
    split_a = 2 * A_WIDTH
    split_b = split_a + 3 * B_WIDTH
    split_c = split_b + 3 * C_WIDTH
    row = lambda p: p.reshape(1, -1)

    x = x.reshape(n, d)
    for l in range(depth):
        x = _ffn(x, row(ffn1_pre_g[l]), _interleave_gate_up(ffn1_wi[l]), ffn1_wo[l].astype(BF16),
                 row(ffn1_post_g[l]))

        w = w_in[l].astype(BF16)
        za, qkvb, qkvc, gate = _proj(x, row(mix_pre_g[l]), w[:, :split_a], w[:, split_a:split_b],
                                     w[:, split_b:split_c], w[:, split_c:])
        qkvb = qkvb.reshape(bsz, s, 3 * B_WIDTH)
        b_parts = []
        for g in range(B_GROUPS):
            o, lse = _band_attention(qkvb, g)
            b_parts.append((o.reshape(n, B_GROUP_WIDTH), lse.reshape(n, B_GROUP_WIDTH)))
        yc = _stick_breaking(qkvc.reshape(bsz, s, 3 * C_WIDTH)).reshape(n, C_WIDTH)

        bs_full = jnp.repeat(a_bs[l].T, A_GROUP_DIM, axis=1)
        x = _merge(x, za, gate, b_parts, yc, row(a_ln_g[l]), row(a_ln_b[l]), a_ws[l], bs_full,
                   w_pa[l].astype(BF16), w_pb[l].astype(BF16), w_pc[l].astype(BF16), w_o[l].astype(BF16),
                   row(mix_post_g[l]))

        x = _ffn(x, row(ffn2_pre_g[l]), _interleave_gate_up(ffn2_wi[l]), ffn2_wo[l].astype(BF16),
                 row(ffn2_post_g[l]))
    return x.reshape(bsz, s, d)
```

```python
import functools

import jax
import jax.numpy as jnp
from jax import lax
from jax.experimental import pallas as pl
from jax.experimental.pallas import tpu as pltpu

D_MODEL = 1024
HEAD_DIM = 64
A_GROUPS = 4
A_CHUNK = 128
A_GROUP_DIM = 128
A_WIDTH = A_GROUPS * A_GROUP_DIM
B_PATTERNS = ((128, 1), (512, 4), (2048, 16))
B_GROUPS = len(B_PATTERNS)
B_HEADS_PER_GROUP = 4
B_GROUP_WIDTH = B_HEADS_PER_GROUP * HEAD_DIM
B_WIDTH = B_GROUPS * B_GROUP_WIDTH
B_BLOCK = 128
B_TASKS = 2
C_HEADS = 8
C_WIDTH = C_HEADS * HEAD_DIM
C_BLOCK = 256
C_HEADS_PER_STEP = 4
D_FF = 2816
EPS = 1e-6
SCALE = HEAD_DIM ** -0.5
LOG2E = 1.4426950408889634
C_Q_SCALE = -SCALE * LOG2E

F32 = jnp.float32
BF16 = jnp.bfloat16

VMEM_LIMIT_BYTES = 56 * 1024 * 1024

FFN_ROWS = 1024
FFN_COLS = 256
PROJ_ROWS = 256
MERGE_ROWS = 256


def _params(*semantics):
    return pltpu.CompilerParams(dimension_semantics=semantics, vmem_limit_bytes=VMEM_LIMIT_BYTES)


def _rms(x, g):
    return x * lax.rsqrt(jnp.mean(x * x, axis=-1, keepdims=True) + EPS) * g


def _resident(shape):
    return pl.BlockSpec(shape, lambda *_: (0,) * len(shape), pipeline_mode=pl.Buffered(1))


def _ffn_kernel(x_ref, pre_g_ref, wg_ref, wu_ref, wo_ref, post_g_ref, o_ref, h_ref, acc_ref):
    c = pl.program_id(1)

    @pl.when(c == 0)
    def _():
        h_ref[...] = _rms(x_ref[...], pre_g_ref[...]).astype(BF16)
        acc_ref[...] = jnp.zeros_like(acc_ref)

    gate = jnp.dot(h_ref[...], wg_ref[...], preferred_element_type=F32)
    up = jnp.dot(h_ref[...], wu_ref[...], preferred_element_type=F32)
    act = (gate * jax.nn.sigmoid(gate) * up).astype(BF16)
    acc_ref[...] += jnp.dot(act, wo_ref[...], preferred_element_type=F32)

    @pl.when(c == pl.num_programs(1) - 1)
    def _():
        o_ref[...] = x_ref[...] + 0.5 * _rms(acc_ref[...], post_g_ref[...])


def _ffn(x, pre_g, wi, wo, post_g):
    n = x.shape[0]
    n_chunks = D_FF // FFN_COLS
    grid = (n // FFN_ROWS, n_chunks)
    return pl.pallas_call(
        _ffn_kernel,
        grid=grid,
        in_specs=[
            pl.BlockSpec((FFN_ROWS, D_MODEL), lambda i, c: (i, 0)),
            pl.BlockSpec((1, D_MODEL), lambda i, c: (0, 0)),
            pl.BlockSpec((D_MODEL, FFN_COLS), lambda i, c: (0, c)),
            pl.BlockSpec((D_MODEL, FFN_COLS), lambda i, c: (0, n_chunks + c)),
            pl.BlockSpec((FFN_COLS, D_MODEL), lambda i, c: (c, 0)),
            pl.BlockSpec((1, D_MODEL), lambda i, c: (0, 0)),
        ],
        out_specs=pl.BlockSpec((FFN_ROWS, D_MODEL), lambda i, c: (i, 0)),
        out_shape=jax.ShapeDtypeStruct((n, D_MODEL), F32),
        scratch_shapes=[pltpu.VMEM((FFN_ROWS, D_MODEL), BF16), pltpu.VMEM((FFN_ROWS, D_MODEL), F32)],
        compiler_params=_params("parallel", "arbitrary"),
        name="ffn",
    )(x, pre_g, wi, wi, wo, post_g)


def _proj_kernel(x_ref, g_ref, wa_ref, wb0_ref, wb1_ref, wb2_ref, wc_ref, wg_ref,
                 za_ref, qkvb0_ref, qkvb1_ref, qkvb2_ref, qkvc_ref, gate_ref):
    h = _rms(x_ref[...], g_ref[...]).astype(BF16)
    za_ref[...] = jnp.dot(h, wa_ref[...], preferred_element_type=F32)
    for w_ref, out_ref in ((wb0_ref, qkvb0_ref), (wb1_ref, qkvb1_ref), (wb2_ref, qkvb2_ref)):
        out_ref[...] = jnp.dot(h, w_ref[...], preferred_element_type=F32).astype(BF16)
    qkvc = jnp.dot(h, wc_ref[...], preferred_element_type=F32)
    qkvc_ref[:, :C_WIDTH] = (qkvc[:, :C_WIDTH] * C_Q_SCALE).astype(BF16)
    qkvc_ref[:, C_WIDTH:] = qkvc[:, C_WIDTH:].astype(BF16)
    gate_ref[...] = jnp.dot(h, wg_ref[...], preferred_element_type=F32)


def _proj(x, g, wa, wbs, wc, wg):
    n = x.shape[0]
    rows = lambda w: pl.BlockSpec((PROJ_ROWS, w), lambda i: (i, 0))
    weights = [wa, *wbs, wc, wg]
    out_dtypes = [F32, BF16, BF16, BF16, BF16, F32]
    return pl.pallas_call(
        _proj_kernel,
        grid=(n // PROJ_ROWS,),
        in_specs=[rows(D_MODEL), _resident((1, D_MODEL))] + [_resident(w.shape) for w in weights],
        out_specs=[rows(w.shape[1]) for w in weights],
        out_shape=[jax.ShapeDtypeStruct((n, w.shape[1]), dt) for w, dt in zip(weights, out_dtypes)],
        compiler_params=_params("parallel"),
        name="proj",
    )(x, g, *weights)


def _band_attention_kernel(qkv_ref, o_ref, lse_ref, *, dil):
    length = qkv_ref.shape[0]
    n_blocks = length // B_BLOCK
    window = min(2 * B_BLOCK, length)
    r_idx = lax.broadcasted_iota(jnp.int32, (B_BLOCK, window), 0)
    c_idx = lax.broadcasted_iota(jnp.int32, (B_BLOCK, window), 1)
    first_head = lax.broadcasted_iota(jnp.int32, (B_BLOCK, 2 * HEAD_DIM), 1) < HEAD_DIM
    contract_last = (((1,), (1,)), ((), ()))
    pair_width = 2 * HEAD_DIM
    n_pairs = B_GROUP_WIDTH // pair_width

    def blocks(tasks):
        qs, ks, vs, valids, dests = [], [], [], [], []
        for r, i in tasks:
            base = r * 3 * B_GROUP_WIDTH
            cur = pl.ds(pl.multiple_of(i * B_BLOCK, B_BLOCK), B_BLOCK)
            start = jnp.maximum(i - 1, 0) * B_BLOCK if n_blocks > 1 else 0
            keys = pl.ds(pl.multiple_of(start, B_BLOCK), window)
            ahead = r_idx + (i * B_BLOCK - start) - c_idx
            valid = jnp.logical_and(ahead >= 0, ahead <= B_BLOCK)
            for p in range(n_pairs):
                lanes = lambda part: slice(base + part * B_GROUP_WIDTH + p * pair_width,
                                           base + part * B_GROUP_WIDTH + (p + 1) * pair_width)
                q2 = qkv_ref[cur, lanes(0)] * SCALE
                zero = jnp.zeros_like(q2)
                qs += [jnp.where(first_head, q2, zero), jnp.where(first_head, zero, q2)]
                ks += [qkv_ref[keys, lanes(1)]] * 2
                vs += [qkv_ref[keys, lanes(2)]] * 2
                valids += [valid] * 2
                dests.append((cur, slice(r * B_GROUP_WIDTH + p * pair_width,
                                         r * B_GROUP_WIDTH + (p + 1) * pair_width)))
        scores = [lax.dot_general(q, k, contract_last, preferred_element_type=F32) for q, k in zip(qs, ks)]
        probs, stats = [], []
        for s, valid in zip(scores, valids):
            s = jnp.where(valid, s, -jnp.inf)
            m = jnp.max(s, axis=1, keepdims=True)
            e = jnp.exp(s - m)
            l = jnp.sum(e, axis=1, keepdims=True)
            probs.append(e.astype(BF16))
            stats.append((l, m + jnp.log(l)))
        outs = [jnp.dot(p, v, preferred_element_type=F32) / l for p, v, (l, _) in zip(probs, vs, stats)]
        for d, (rows, lanes) in enumerate(dests):
            o_ref[rows, lanes] = jnp.where(first_head, outs[2 * d], outs[2 * d + 1])
            lse_ref[rows, lanes] = jnp.where(first_head, stats[2 * d][1], stats[2 * d + 1][1])

    if n_blocks == 1:
        for r in range(0, dil, B_TASKS):
            blocks([(r + t, 0) for t in range(B_TASKS)])
    else:
        for r in range(dil):
            def body(step, carry, r=r):
                blocks([(r, step * B_TASKS + t) for t in range(B_TASKS)])
                return carry
            lax.fori_loop(0, n_blocks // B_TASKS, body, 0)


def _band_attention(qkvb, group, bsz):
    n, width = qkvb.shape
    dil = B_PATTERNS[group][1]
    length = n // bsz // dil
    view = qkvb.reshape(bsz, length, dil * width)
    out_spec = pl.BlockSpec((None, length, dil * B_GROUP_WIDTH), lambda b: (b, 0, 0))
    out_shape = jax.ShapeDtypeStruct((bsz, length, dil * B_GROUP_WIDTH), F32)
    o, lse = pl.pallas_call(
        functools.partial(_band_attention_kernel, dil=dil),
        grid=(bsz,),
        in_specs=[pl.BlockSpec((None, length, dil * width), lambda b: (b, 0, 0))],
        out_specs=[out_spec, out_spec],
        out_shape=[out_shape, out_shape],
        compiler_params=_params("parallel"),
        name=f"band_attention_g{group}",
    )(view)
    return o.reshape(n, B_GROUP_WIDTH), lse.reshape(n, B_GROUP_WIDTH)


def _stick_breaking_blocks(qs, ks, vs, upper2, mask):
    contract_last = (((1,), (1,)), ((), ()))
    zs = [lax.dot_general(q, k, contract_last, preferred_element_type=F32) for q, k in zip(qs, ks)]
    log_keeps, logits, afters = [], [], []
    for nw in zs:
        neg_abs = lax.bitcast_convert_type(
            lax.bitcast_convert_type(nw, jnp.uint32) | jnp.uint32(0x80000000), F32)
        log_keep = jnp.minimum(nw, 0.0) - jnp.log2(1.0 + jnp.exp2(neg_abs))
        if mask is not None:
            log_keep = jnp.where(mask, log_keep, 0.0)
        hi = log_keep.astype(BF16)
        lo = (log_keep - hi.astype(F32)).astype(BF16)
        afters.append(jnp.dot(jnp.concatenate([hi, lo], axis=1), upper2, preferred_element_type=F32))
        log_keeps.append(log_keep)
        logits.append(log_keep - nw)
    out = []
    for log_keep, logit, after, v in zip(log_keeps, logits, afters, vs):
        a = jnp.exp2(logit + after)
        if mask is not None:
            a = jnp.where(mask, a, 0.0)
        contrib = jnp.dot(a.astype(BF16), v, preferred_element_type=F32)
        out.append((contrib, jnp.sum(log_keep, axis=1, keepdims=True)))
    return out


def _stick_breaking_kernel(q_ref, k_ref, v_ref, o_ref, acc_ref, passed_ref):
    n_blocks = q_ref.shape[0] // C_BLOCK
    row = lax.broadcasted_iota(jnp.int32, (C_BLOCK, C_BLOCK), 0)
    col = lax.broadcasted_iota(jnp.int32, (C_BLOCK, C_BLOCK), 1)
    strictly_lower = col < row
    row2 = lax.broadcasted_iota(jnp.int32, (2 * C_BLOCK, C_BLOCK), 0)
    col2 = lax.broadcasted_iota(jnp.int32, (2 * C_BLOCK, C_BLOCK), 1)
    upper2 = jnp.where((row2 & (C_BLOCK - 1)) > col2, 1.0, 0.0).astype(BF16)
    heads = [slice(h * HEAD_DIM, (h + 1) * HEAD_DIM) for h in range(C_HEADS_PER_STEP)]

    def q_block(i, carry):
        rows_i = pl.ds(pl.multiple_of(i * C_BLOCK, C_BLOCK), C_BLOCK)

        def tiles(rows_j, mask):
            return _stick_breaking_blocks([q_ref[rows_i, c] for c in heads], [k_ref[rows_j, c] for c in heads],
                                          [v_ref[rows_j, c] for c in heads], upper2, mask)

        for h, (contrib, row_sum) in enumerate(tiles(rows_i, strictly_lower)):
            acc_ref[h] = contrib
            passed_ref[h] = jnp.broadcast_to(row_sum, passed_ref.shape[1:])

        def k_block(step, carry):
            rows_j = pl.ds(pl.multiple_of((i - 1 - step) * C_BLOCK, C_BLOCK), C_BLOCK)
            for h, (contrib, row_sum) in enumerate(tiles(rows_j, None)):
                passed = passed_ref[h]
                acc_ref[h] += contrib * jnp.exp2(passed[:, :HEAD_DIM])
                passed_ref[h] = passed + row_sum
            return carry

        lax.fori_loop(0, i, k_block, 0)
        for h, cols in enumerate(heads):
            o_ref[rows_i, cols] = acc_ref[h].astype(o_ref.dtype)
        return carry

    lax.fori_loop(0, n_blocks, q_block, 0)


def _stick_breaking(qkvc):
    bsz, s, _ = qkvc.shape
    width = C_HEADS_PER_STEP * HEAD_DIM
    n_steps = C_WIDTH // width

    def spec(part):
        return pl.BlockSpec((None, s, width), lambda b, p: (b, 0, part * n_steps + p))

    return pl.pallas_call(
        _stick_breaking_kernel,
        grid=(bsz, n_steps),
        in_specs=[spec(0), spec(1), spec(2)],
        out_specs=pl.BlockSpec((None, s, width), lambda b, p: (b, 0, p)),
        out_shape=jax.ShapeDtypeStruct((bsz, s, C_WIDTH), BF16),
        scratch_shapes=[pltpu.VMEM((C_HEADS_PER_STEP, C_BLOCK, HEAD_DIM), F32),
                        pltpu.VMEM((C_HEADS_PER_STEP, C_BLOCK, 128), F32)],
        compiler_params=_params("parallel", "parallel"),
        name="stick_breaking",
    )(qkvc, qkvc, qkvc)


def _merge_kernel(x_ref, za_ref, gate_ref, o0_ref, l0_ref, o1_ref, l1_ref, o2_ref, l2_ref, yc_ref,
                  ln_g_ref, ln_b_ref, ws_ref, bs_ref, wpa_ref, wpb_ref, wpc_ref, wo_ref, post_g_ref,
                  out_ref):
    z = jax.nn.gelu(za_ref[...], approximate=True)
    u = z[:, :A_WIDTH]
    v = z[:, A_WIDTH:]
    mu = jnp.mean(v, axis=-1, keepdims=True)
    vc = v - mu
    v = vc * lax.rsqrt(jnp.mean(vc * vc, axis=-1, keepdims=True) + EPS) * ln_g_ref[...] + ln_b_ref[...]
    v = v.astype(BF16)
    t_idx = lax.broadcasted_iota(jnp.int32, (A_CHUNK, A_CHUNK), 0)
    s_idx = lax.broadcasted_iota(jnp.int32, (A_CHUNK, A_CHUNK), 1)
    chunks = []
    for c in range(MERGE_ROWS // A_CHUNK):
        groups = []
        for g in range(A_GROUPS):
            w = jnp.where(s_idx <= t_idx, ws_ref[g], 0.0).astype(BF16)
            vg = v[c * A_CHUNK:(c + 1) * A_CHUNK, g * A_GROUP_DIM:(g + 1) * A_GROUP_DIM]
            groups.append(jnp.dot(w, vg, preferred_element_type=F32))
        chunks.append(jnp.concatenate(groups, axis=1) + bs_ref[...])
    y_a = u * jnp.concatenate(chunks, axis=0)

    l0, l1, l2 = l0_ref[...], l1_ref[...], l2_ref[...]
    m = jnp.maximum(jnp.maximum(l0, l1), l2)
    w0, w1, w2 = jnp.exp(l0 - m), jnp.exp(l1 - m), jnp.exp(l2 - m)
    y_b = (w0 * o0_ref[...] + w1 * o1_ref[...] + w2 * o2_ref[...]) / (w0 + w1 + w2)

    gates = jax.nn.sigmoid(gate_ref[...])
    merged = gates[:, :D_MODEL] * jnp.dot(y_a.astype(BF16), wpa_ref[...], preferred_element_type=F32)
    merged += gates[:, D_MODEL:2 * D_MODEL] * jnp.dot(y_b.astype(BF16), wpb_ref[...], preferred_element_type=F32)
    merged += gates[:, 2 * D_MODEL:] * jnp.dot(yc_ref[...], wpc_ref[...], preferred_element_type=F32)
    y = jnp.dot(merged.astype(BF16), wo_ref[...], preferred_element_type=F32)
    out_ref[...] = x_ref[...] + _rms(y, post_g_ref[...])


def _merge(x, za, gate, b_parts, yc, ln_g, ln_b, ws, bs_full, wpa, wpb, wpc, wo, post_g):
    n = x.shape[0]
    rows = lambda w: pl.BlockSpec((MERGE_ROWS, w), lambda i: (i, 0))
    b_flat = [a for part in b_parts for a in part]
    return pl.pallas_call(
        _merge_kernel,
        grid=(n // MERGE_ROWS,),
        in_specs=[rows(D_MODEL), rows(2 * A_WIDTH), rows(3 * D_MODEL)]
        + [rows(B_GROUP_WIDTH)] * 6
        + [rows(C_WIDTH), _resident(ln_g.shape), _resident(ln_b.shape), _resident(ws.shape),
           _resident(bs_full.shape), _resident(wpa.shape), _resident(wpb.shape), _resident(wpc.shape),
           _resident(wo.shape), _resident(post_g.shape)],
        out_specs=rows(D_MODEL),
        out_shape=jax.ShapeDtypeStruct((n, D_MODEL), F32),
        compiler_params=_params("parallel"),
        name="merge",
    )(x, za, gate, *b_flat, yc, ln_g, ln_b, ws, bs_full, wpa, wpb, wpc, wo, post_g)


def kernel(x, ffn1_pre_g, ffn1_wi, ffn1_wo, ffn1_post_g, mix_pre_g, w_in, a_ln_g, a_ln_b, a_ws, a_bs,
           w_pa, w_pb, w_pc, w_o, mix_post_g, ffn2_pre_g, ffn2_wi, ffn2_wo, ffn2_post_g):
    bsz, s, d = x.shape
    n = bsz * s
    depth = ffn1_wi.shape[0]
    split_a = 2 * A_WIDTH
    split_b = split_a + 3 * B_WIDTH
    split_c = split_b + 3 * C_WIDTH
    row = lambda p: p.reshape(1, -1)

    x = x.reshape(n, d)
    for l in range(depth):
        x = _ffn(x, row(ffn1_pre_g[l]), ffn1_wi[l].astype(BF16), ffn1_wo[l].astype(BF16),
                 row(ffn1_post_g[l]))

        w = w_in[l].astype(BF16)
        wb = w[:, split_a:split_b].reshape(d, 3, B_GROUPS, B_GROUP_WIDTH)
        wbs = [wb[:, :, g].reshape(d, 3 * B_GROUP_WIDTH) for g in range(B_GROUPS)]
        za, *qkvbs, qkvc, gate = _proj(x, row(mix_pre_g[l]), w[:, :split_a], wbs,
                                       w[:, split_b:split_c], w[:, split_c:])
        b_parts = [_band_attention(qkvb, g, bsz) for g, qkvb in enumerate(qkvbs)]
        yc = _stick_breaking(qkvc.reshape(bsz, s, 3 * C_WIDTH)).reshape(n, C_WIDTH)

        bs_full = jnp.repeat(a_bs[l].T, A_GROUP_DIM, axis=1)
        x = _merge(x, za, gate, b_parts, yc, row(a_ln_g[l]), row(a_ln_b[l]), a_ws[l], bs_full,
                   w_pa[l].astype(BF16), w_pb[l].astype(BF16), w_pc[l].astype(BF16), w_o[l].astype(BF16),
                   row(mix_post_g[l]))

        x = _ffn(x, row(ffn2_pre_g[l]), ffn2_wi[l].astype(BF16), ffn2_wo[l].astype(BF16),
                 row(ffn2_post_g[l]))
    return x.reshape(bsz, s, d)
```

```python
import functools

import jax
import jax.numpy as jnp
from jax import lax
from jax.experimental import pallas as pl
from jax.experimental.pallas import tpu as pltpu

D_MODEL = 1024
HEAD_DIM = 64
A_GROUPS = 4
A_CHUNK = 128
A_GROUP_DIM = 128
A_WIDTH = A_GROUPS * A_GROUP_DIM
B_PATTERNS = ((128, 1), (512, 4), (2048, 16))
B_GROUPS = len(B_PATTERNS)
B_HEADS_PER_GROUP = 4
B_GROUP_WIDTH = B_HEADS_PER_GROUP * HEAD_DIM
B_WIDTH = B_GROUPS * B_GROUP_WIDTH
B_BLOCK = 128
B_TASKS = 2
C_HEADS = 8
C_WIDTH = C_HEADS * HEAD_DIM
C_BLOCK = 256
C_HEADS_PER_STEP = 4
D_FF = 2816
EPS = 1e-6
SCALE = HEAD_DIM ** -0.5
LOG2E = 1.4426950408889634
C_Q_SCALE = -SCALE * LOG2E
C_DEAD_LOG2 = -160.0

F32 = jnp.float32
BF16 = jnp.bfloat16

LANES = 128

VMEM_LIMIT_BYTES = 56 * 1024 * 1024

FFN_ROWS = 1024
FFN_COLS = 256
PROJ_ROWS = 256
MERGE_ROWS = 256


def _params(*semantics):
    return pltpu.CompilerParams(dimension_semantics=semantics, vmem_limit_bytes=VMEM_LIMIT_BYTES)


def _rms(x, g):
    return x * lax.rsqrt(jnp.mean(x * x, axis=-1, keepdims=True) + EPS) * g


def _resident(shape):
    return pl.BlockSpec(shape, lambda *_: (0,) * len(shape), pipeline_mode=pl.Buffered(1))


def _ffn_kernel(x_ref, pre_g_ref, wg_ref, wu_ref, wo_ref, post_g_ref, o_ref, h_ref, acc_ref):
    c = pl.program_id(1)

    @pl.when(c == 0)
    def _():
        h_ref[...] = _rms(x_ref[...], pre_g_ref[...]).astype(BF16)
        acc_ref[...] = jnp.zeros_like(acc_ref)

    gate = jnp.dot(h_ref[...], wg_ref[...], preferred_element_type=F32)
    up = jnp.dot(h_ref[...], wu_ref[...], preferred_element_type=F32)
    act = (gate * jax.nn.sigmoid(gate) * up).astype(BF16)
    acc_ref[...] += jnp.dot(act, wo_ref[...], preferred_element_type=F32)

    @pl.when(c == pl.num_programs(1) - 1)
    def _():
        o_ref[...] = x_ref[...] + 0.5 * _rms(acc_ref[...], post_g_ref[...])


def _ffn(x, pre_g, wi, wo, post_g):
    n = x.shape[0]
    n_chunks = D_FF // FFN_COLS
    grid = (n // FFN_ROWS, n_chunks)
    return pl.pallas_call(
        _ffn_kernel,
        grid=grid,
        in_specs=[
            pl.BlockSpec((FFN_ROWS, D_MODEL), lambda i, c: (i, 0)),
            pl.BlockSpec((1, D_MODEL), lambda i, c: (0, 0)),
            pl.BlockSpec((D_MODEL, FFN_COLS), lambda i, c: (0, c)),
            pl.BlockSpec((D_MODEL, FFN_COLS), lambda i, c: (0, n_chunks + c)),
            pl.BlockSpec((FFN_COLS, D_MODEL), lambda i, c: (c, 0)),
            pl.BlockSpec((1, D_MODEL), lambda i, c: (0, 0)),
        ],
        out_specs=pl.BlockSpec((FFN_ROWS, D_MODEL), lambda i, c: (i, 0)),
        out_shape=jax.ShapeDtypeStruct((n, D_MODEL), F32),
        scratch_shapes=[pltpu.VMEM((FFN_ROWS, D_MODEL), BF16), pltpu.VMEM((FFN_ROWS, D_MODEL), F32)],
        compiler_params=_params("parallel", "arbitrary"),
        name="ffn",
    )(x, pre_g, wi, wi, wo, post_g)


def _proj_kernel(x_ref, g_ref, wa_ref, wb0_ref, wb1_ref, wb2_ref, wc_ref, wg_ref,
                 za_ref, qkvb0_ref, qkvb1_ref, qkvb2_ref, qkvc_ref, gate_ref, class_ref):
    h = _rms(x_ref[...], g_ref[...]).astype(BF16)
    za_ref[...] = jnp.dot(h, wa_ref[...], preferred_element_type=F32)
    groups = ((wb0_ref, qkvb0_ref), (wb1_ref, qkvb1_ref), (wb2_ref, qkvb2_ref))
    for (w_ref, out_ref), (_, dil) in zip(groups, B_PATTERNS):
        qkv = jnp.dot(h, w_ref[...], preferred_element_type=F32)
        if dil == 1:
            out_ref[...] = qkv.astype(BF16)
            continue
        width = qkv.shape[1]
        for c in range(width // LANES):
            class_ref[c] = qkv[:, c * LANES:(c + 1) * LANES]
        for r in range(dil):
            for c in range(width // LANES):
                lanes = slice(r * width + c * LANES, r * width + (c + 1) * LANES)
                out_ref[:, lanes] = class_ref[c, pl.ds(r, PROJ_ROWS // dil, stride=dil), :].astype(BF16)
    qkvc = jnp.dot(h, wc_ref[...], preferred_element_type=F32)
    qkvc_ref[:, :C_WIDTH] = (qkvc[:, :C_WIDTH] * C_Q_SCALE).astype(BF16)
    qkvc_ref[:, C_WIDTH:] = qkvc[:, C_WIDTH:].astype(BF16)
    gate_ref[...] = jnp.dot(h, wg_ref[...], preferred_element_type=F32)


def _proj(x, g, wa, wbs, wc, wg):
    n = x.shape[0]
    rows = lambda w, dil=1: pl.BlockSpec((PROJ_ROWS // dil, dil * w), lambda i: (i, 0))
    weights = [wa, *wbs, wc, wg]
    out_dtypes = [F32, BF16, BF16, BF16, BF16, F32]
    dils = [1, *(dil for _, dil in B_PATTERNS), 1, 1]
    return pl.pallas_call(
        _proj_kernel,
        grid=(n // PROJ_ROWS,),
        in_specs=[rows(D_MODEL), _resident((1, D_MODEL))] + [_resident(w.shape) for w in weights],
        out_specs=[rows(w.shape[1], dil) for w, dil in zip(weights, dils)],
        out_shape=[jax.ShapeDtypeStruct((n // dil, dil * w.shape[1]), dt)
                   for w, dt, dil in zip(weights, out_dtypes, dils)],
        scratch_shapes=[pltpu.VMEM((3 * B_GROUP_WIDTH // LANES, PROJ_ROWS, LANES), F32)],
        compiler_params=_params("parallel"),
        name="proj",
    )(x, g, *weights)


def _band_attention_kernel(qkv_ref, o_ref, lse_ref, *, dil):
    length = qkv_ref.shape[0]
    n_blocks = length // B_BLOCK
    window = min(2 * B_BLOCK, length)
    r_idx = lax.broadcasted_iota(jnp.int32, (B_BLOCK, window), 0)
    c_idx = lax.broadcasted_iota(jnp.int32, (B_BLOCK, window), 1)
    first_head = lax.broadcasted_iota(jnp.int32, (B_BLOCK, 2 * HEAD_DIM), 1) < HEAD_DIM
    contract_last = (((1,), (1,)), ((), ()))
    pair_width = 2 * HEAD_DIM
    n_pairs = B_GROUP_WIDTH // pair_width

    def blocks(tasks):
        qs, ks, vs, valids, dests = [], [], [], [], []
        for r, i in tasks:
            base = r * 3 * B_GROUP_WIDTH
            cur = pl.ds(pl.multiple_of(i * B_BLOCK, B_BLOCK), B_BLOCK)
            start = jnp.maximum(i - 1, 0) * B_BLOCK if n_blocks > 1 else 0
            keys = pl.ds(pl.multiple_of(start, B_BLOCK), window)
            ahead = r_idx + (i * B_BLOCK - start) - c_idx
            valid = jnp.logical_and(ahead >= 0, ahead <= B_BLOCK)
            for p in range(n_pairs):
                lanes = lambda part: slice(base + part * B_GROUP_WIDTH + p * pair_width,
                                           base + part * B_GROUP_WIDTH + (p + 1) * pair_width)
                q2 = qkv_ref[cur, lanes(0)] * SCALE
                zero = jnp.zeros_like(q2)
                qs += [jnp.where(first_head, q2, zero), jnp.where(first_head, zero, q2)]
                ks += [qkv_ref[keys, lanes(1)]] * 2
                vs += [qkv_ref[keys, lanes(2)]] * 2
                valids += [valid] * 2
                dests.append((cur, slice(r * B_GROUP_WIDTH + p * pair_width,
                                         r * B_GROUP_WIDTH + (p + 1) * pair_width)))
        scores = [lax.dot_general(q, k, contract_last, preferred_element_type=F32) for q, k in zip(qs, ks)]
        probs, stats = [], []
        for s, valid in zip(scores, valids):
            s = jnp.where(valid, s, -jnp.inf)
            m = jnp.max(s, axis=1, keepdims=True)
            e = jnp.exp(s - m)
            l = jnp.sum(e, axis=1, keepdims=True)
            probs.append(e.astype(BF16))
            stats.append((l, m + jnp.log(l)))
        outs = [jnp.dot(p, v, preferred_element_type=F32) / l for p, v, (l, _) in zip(probs, vs, stats)]
        for d, (rows, lanes) in enumerate(dests):
            o_ref[rows, lanes] = jnp.where(first_head, outs[2 * d], outs[2 * d + 1])
            lse_ref[rows, lanes] = jnp.where(first_head, stats[2 * d][1], stats[2 * d + 1][1])

    if n_blocks == 1:
        for r in range(0, dil, B_TASKS):
            blocks([(r + t, 0) for t in range(B_TASKS)])
    else:
        for r in range(dil):
            def body(step, carry, r=r):
                blocks([(r, step * B_TASKS + t) for t in range(B_TASKS)])
                return carry
            lax.fori_loop(0, n_blocks // B_TASKS, body, 0)


def _band_attention(qkvb, group, bsz):
    dil = B_PATTERNS[group][1]
    length = qkvb.shape[0] // bsz
    width = qkvb.shape[1] // dil
    view = qkvb.reshape(bsz, length, dil * width)
    out_spec = pl.BlockSpec((None, length, dil * B_GROUP_WIDTH), lambda b: (b, 0, 0))
    out_shape = jax.ShapeDtypeStruct((bsz, length, dil * B_GROUP_WIDTH), F32)
    o, lse = pl.pallas_call(
        functools.partial(_band_attention_kernel, dil=dil),
        grid=(bsz,),
        in_specs=[pl.BlockSpec((None, length, dil * width), lambda b: (b, 0, 0))],
        out_specs=[out_spec, out_spec],
        out_shape=[out_shape, out_shape],
        compiler_params=_params("parallel"),
        name=f"band_attention_g{group}",
    )(view)
    return o.reshape(bsz * length, dil * B_GROUP_WIDTH), lse.reshape(bsz * length, dil * B_GROUP_WIDTH)


def _stick_breaking_tiles(qs, ks, vs, upper, mask):
    contract_last = (((1,), (1,)), ((), ()))
    zs = [lax.dot_general(q, k, contract_last, preferred_element_type=F32) for q, k in zip(qs, ks)]
    log_keeps, logits, afters = [], [], []
    for nw in zs:
        log_keep = jnp.minimum(nw, 0.0) - jnp.log2(1.0 + jnp.exp2(-jnp.abs(nw)))
        if mask is not None:
            log_keep = jnp.where(mask, log_keep, 0.0)
        afters.append(jnp.dot(log_keep.astype(BF16), upper, preferred_element_type=F32))
        log_keeps.append(log_keep)
        logits.append(log_keep - nw)
    out = []
    for log_keep, logit, after, v in zip(log_keeps, logits, afters, vs):
        a = jnp.exp2(logit + after)
        if mask is not None:
            a = jnp.where(mask, a, 0.0)
        contrib = jnp.dot(a.astype(BF16), v, preferred_element_type=F32)
        out.append((contrib, jnp.sum(log_keep, axis=1, keepdims=True)))
    return out


def _stick_breaking_kernel(q_ref, k_ref, v_ref, o_ref, acc_ref, passed_ref):
    n_blocks = q_ref.shape[0] // C_BLOCK
    row = lax.broadcasted_iota(jnp.int32, (C_BLOCK, C_BLOCK), 0)
    col = lax.broadcasted_iota(jnp.int32, (C_BLOCK, C_BLOCK), 1)
    strictly_lower = col < row
    upper = jnp.where(row > col, 1.0, 0.0).astype(BF16)
    pair_width = 2 * HEAD_DIM
    pairs = [slice(p * pair_width, (p + 1) * pair_width) for p in range(C_HEADS_PER_STEP // 2)]
    first_head = lax.broadcasted_iota(jnp.int32, (C_BLOCK, pair_width), 1) < HEAD_DIM

    def q_block(i, carry):
        rows_i = pl.ds(pl.multiple_of(i * C_BLOCK, C_BLOCK), C_BLOCK)

        def tiles(rows_j, mask):
            qs, ks, vs = [], [], []
            for lanes in pairs:
                q2 = q_ref[rows_i, lanes]
                zero = jnp.zeros_like(q2)
                qs += [jnp.where(first_head, q2, zero), jnp.where(first_head, zero, q2)]
                ks += [k_ref[rows_j, lanes]] * 2
                vs += [v_ref[rows_j, lanes]] * 2
            res = _stick_breaking_tiles(qs, ks, vs, upper, mask)
            return [(jnp.where(first_head, res[2 * p][0], res[2 * p + 1][0]),
                     jnp.where(first_head, res[2 * p][1], res[2 * p + 1][1])) for p in range(len(pairs))]

        most_alive = None
        for p, (contrib, row_sum) in enumerate(tiles(rows_i, strictly_lower)):
            acc_ref[p] = contrib
            passed_ref[p] = row_sum
            most_alive = row_sum if most_alive is None else jnp.maximum(most_alive, row_sum)

        def k_block(state):
            step, _ = state
            rows_j = pl.ds(pl.multiple_of((i - 1 - step) * C_BLOCK, C_BLOCK), C_BLOCK)
            most_alive = None
            for p, (contrib, row_sum) in enumerate(tiles(rows_j, None)):
                passed = passed_ref[p]
                acc_ref[p] += contrib * jnp.exp2(passed)
                passed = passed + row_sum
                passed_ref[p] = passed
                most_alive = passed if most_alive is None else jnp.maximum(most_alive, passed)
            return step + 1, jnp.max(most_alive)

        def more_to_do(state):
            step, most_alive = state
            return jnp.logical_and(step < i, most_alive > C_DEAD_LOG2)

        lax.while_loop(more_to_do, k_block, (jnp.int32(0), jnp.max(most_alive)))
        for p, lanes in enumerate(pairs):
            o_ref[rows_i, lanes] = acc_ref[p].astype(o_ref.dtype)
        return carry

    lax.fori_loop(0, n_blocks, q_block, 0)


def _stick_breaking(qkvc):
    bsz, s, _ = qkvc.shape
    width = C_HEADS_PER_STEP * HEAD_DIM
    n_steps = C_WIDTH // width

    def spec(part):
        return pl.BlockSpec((None, s, width), lambda b, p: (b, 0, part * n_steps + p))

    return pl.pallas_call(
        _stick_breaking_kernel,
        grid=(bsz, n_steps),
        in_specs=[spec(0), spec(1), spec(2)],
        out_specs=pl.BlockSpec((None, s, width), lambda b, p: (b, 0, p)),
        out_shape=jax.ShapeDtypeStruct((bsz, s, C_WIDTH), BF16),
        scratch_shapes=[pltpu.VMEM((C_HEADS_PER_STEP // 2, C_BLOCK, 2 * HEAD_DIM), F32),
                        pltpu.VMEM((C_HEADS_PER_STEP // 2, C_BLOCK, 2 * HEAD_DIM), F32)],
        compiler_params=_params("parallel", "parallel"),
        name="stick_breaking",
    )(qkvc, qkvc, qkvc)


def _token_order(src_ref, scratch_ref, dil):
    if dil == 1:
        return src_ref[...]
    tiles, rows, _ = scratch_ref.shape
    for r in range(dil):
        for c in range(tiles):
            lanes = slice((r * tiles + c) * LANES, (r * tiles + c + 1) * LANES)
            scratch_ref[c, pl.ds(r, rows // dil, stride=dil), :] = src_ref[:, lanes]
    return jnp.concatenate([scratch_ref[c] for c in range(tiles)], axis=1)


def _merge_kernel(x_ref, za_ref, gate_ref, o0_ref, l0_ref, o1_ref, l1_ref, o2_ref, l2_ref, yc_ref,
                  ln_g_ref, ln_b_ref, ws_ref, bs_ref, wpa_ref, wpb_ref, wpc_ref, wo_ref, post_g_ref,
                  out_ref, *order_refs):
    z = jax.nn.gelu(za_ref[...], approximate=True)
    u = z[:, :A_WIDTH]
    v = z[:, A_WIDTH:]
    mu = jnp.mean(v, axis=-1, keepdims=True)
    vc = v - mu
    v = vc * lax.rsqrt(jnp.mean(vc * vc, axis=-1, keepdims=True) + EPS) * ln_g_ref[...] + ln_b_ref[...]
    v = v.astype(BF16)
    t_idx = lax.broadcasted_iota(jnp.int32, (A_CHUNK, A_CHUNK), 0)
    s_idx = lax.broadcasted_iota(jnp.int32, (A_CHUNK, A_CHUNK), 1)
    chunks = []
    for c in range(MERGE_ROWS // A_CHUNK):
        groups = []
        for g in range(A_GROUPS):
            w = jnp.where(s_idx <= t_idx, ws_ref[g], 0.0).astype(BF16)
            vg = v[c * A_CHUNK:(c + 1) * A_CHUNK, g * A_GROUP_DIM:(g + 1) * A_GROUP_DIM]
            groups.append(jnp.dot(w, vg, preferred_element_type=F32))
        chunks.append(jnp.concatenate(groups, axis=1) + bs_ref[...])
    y_a = u * jnp.concatenate(chunks, axis=0)

    dils = [dil for _, dil in B_PATTERNS]
    o0, o1, o2 = [_token_order(src, order_refs[2 * g], dils[g]) for g, src in enumerate((o0_ref, o1_ref, o2_ref))]
    l0, l1, l2 = [_token_order(src, order_refs[2 * g + 1], dils[g]) for g, src in enumerate((l0_ref, l1_ref, l2_ref))]
    m = jnp.maximum(jnp.maximum(l0, l1), l2)
    w0, w1, w2 = jnp.exp(l0 - m), jnp.exp(l1 - m), jnp.exp(l2 - m)
    y_b = (w0 * o0 + w1 * o1 + w2 * o2) / (w0 + w1 + w2)

    gates = jax.nn.sigmoid(gate_ref[...])
    merged = gates[:, :D_MODEL] * jnp.dot(y_a.astype(BF16), wpa_ref[...], preferred_element_type=F32)
    merged += gates[:, D_MODEL:2 * D_MODEL] * jnp.dot(y_b.astype(BF16), wpb_ref[...], preferred_element_type=F32)
    merged += gates[:, 2 * D_MODEL:] * jnp.dot(yc_ref[...], wpc_ref[...], preferred_element_type=F32)
    y = jnp.dot(merged.astype(BF16), wo_ref[...], preferred_element_type=F32)
    out_ref[...] = x_ref[...] + _rms(y, post_g_ref[...])


def _merge(x, za, gate, b_parts, yc, ln_g, ln_b, ws, bs_full, wpa, wpb, wpc, wo, post_g):
    n = x.shape[0]
    rows = lambda w: pl.BlockSpec((MERGE_ROWS, w), lambda i: (i, 0))
    b_flat = [a for part in b_parts for a in part]
    b_specs = [pl.BlockSpec((MERGE_ROWS // dil, dil * B_GROUP_WIDTH), lambda i: (i, 0))
               for _, dil in B_PATTERNS for _ in range(2)]
    return pl.pallas_call(
        _merge_kernel,
        grid=(n // MERGE_ROWS,),
        in_specs=[rows(D_MODEL), rows(2 * A_WIDTH), rows(3 * D_MODEL)]
        + b_specs
        + [rows(C_WIDTH), _resident(ln_g.shape), _resident(ln_b.shape), _resident(ws.shape),
           _resident(bs_full.shape), _resident(wpa.shape), _resident(wpb.shape), _resident(wpc.shape),
           _resident(wo.shape), _resident(post_g.shape)],
        out_specs=rows(D_MODEL),
        out_shape=jax.ShapeDtypeStruct((n, D_MODEL), F32),
        scratch_shapes=[pltpu.VMEM((B_GROUP_WIDTH // LANES, MERGE_ROWS, LANES), F32)] * (2 * B_GROUPS),
        compiler_params=_params("parallel"),
        name="merge",
    )(x, za, gate, *b_flat, yc, ln_g, ln_b, ws, bs_full, wpa, wpb, wpc, wo, post_g)


def kernel(x, ffn1_pre_g, ffn1_wi, ffn1_wo, ffn1_post_g, mix_pre_g, w_in, a_ln_g, a_ln_b, a_ws, a_bs,
           w_pa, w_pb, w_pc, w_o, mix_post_g, ffn2_pre_g, ffn2_wi, ffn2_wo, ffn2_post_g):
    bsz, s, d = x.shape
    n = bsz * s
    depth = ffn1_wi.shape[0]
    split_a = 2 * A_WIDTH
    split_b = split_a + 3 * B_WIDTH
    split_c = split_b + 3 * C_WIDTH
    row = lambda p: p.reshape(1, -1)

    x = x.reshape(n, d)
    for l in range(depth):
        x = _ffn(x, row(ffn1_pre_g[l]), ffn1_wi[l].astype(BF16), ffn1_wo[l].astype(BF16),
                 row(ffn1_post_g[l]))

        w = w_in[l].astype(BF16)
        wb = w[:, split_a:split_b].reshape(d, 3, B_GROUPS, B_GROUP_WIDTH)
        wbs = [wb[:, :, g].reshape(d, 3 * B_GROUP_WIDTH) for g in range(B_GROUPS)]
        za, *qkvbs, qkvc, gate = _proj(x, row(mix_pre_g[l]), w[:, :split_a], wbs,
                                       w[:, split_b:split_c], w[:, split_c:])
        b_parts = [_band_attention(qkvb, g, bsz) for g, qkvb in enumerate(qkvbs)]
        yc = _stick_breaking(qkvc.reshape(bsz, s, 3 * C_WIDTH)).reshape(n, C_WIDTH)

        bs_full = jnp.repeat(a_bs[l].T, A_GROUP_DIM, axis=1)
        x = _merge(x, za, gate, b_parts, yc, row(a_ln_g[l]), row(a_ln_b[l]), a_ws[l], bs_full,
                   w_pa[l].astype(BF16), w_pb[l].astype(BF16), w_pc[l].astype(BF16), w_o[l].astype(BF16),
                   row(mix_post_g[l]))

        x = _ffn(x, row(ffn2_pre_g[l]), ffn2_wi[l].astype(BF16), ffn2_wo[l].astype(BF16),
                 row(ffn2_post_g[l]))
    return x.reshape(bsz, s, d)
```

```python
import functools

import jax
import jax.numpy as jnp
from jax import lax
from jax.experimental import pallas as pl
from jax.experimental.pallas import tpu as pltpu

D_MODEL = 1024
HEAD_DIM = 64
A_GROUPS = 4
A_CHUNK = 128
A_GROUP_DIM = 128
A_WIDTH = A_GROUPS * A_GROUP_DIM
B_PATTERNS = ((128, 1), (512, 4), (2048, 16))
B_GROUPS = len(B_PATTERNS)
B_HEADS_PER_GROUP = 4
B_GROUP_WIDTH = B_HEADS_PER_GROUP * HEAD_DIM
B_WIDTH = B_GROUPS * B_GROUP_WIDTH
B_BLOCK = 128
B_TASKS = 2
C_HEADS = 8
C_WIDTH = C_HEADS * HEAD_DIM
C_BLOCK = 256
C_HEADS_PER_STEP = 4
D_FF = 2816
EPS = 1e-6
SCALE = HEAD_DIM ** -0.5
LOG2E = 1.4426950408889634
C_Q_SCALE = -SCALE * LOG2E
C_DEAD_LOG2 = -160.0

F32 = jnp.float32
BF16 = jnp.bfloat16

LANES = 128

VMEM_LIMIT_BYTES = 56 * 1024 * 1024

FFN_ROWS = 1024
FFN_COLS = 256
PROJ_ROWS = 512
MERGE_ROWS = 512


def _params(*semantics):
    return pltpu.CompilerParams(dimension_semantics=semantics, vmem_limit_bytes=VMEM_LIMIT_BYTES)


def _rms(x, g):
    return x * lax.rsqrt(jnp.mean(x * x, axis=-1, keepdims=True) + EPS) * g


def _resident(shape):
    return pl.BlockSpec(shape, lambda *_: (0,) * len(shape), pipeline_mode=pl.Buffered(1))


def _ffn_kernel(x_ref, pre_g_ref, wi_ref, wo_ref, post_g_ref, o_ref, act_ref):
    h = _rms(x_ref[...], pre_g_ref[...]).astype(BF16)
    for c in range(D_FF // FFN_COLS):
        gate = jnp.dot(h, wi_ref[:, c * FFN_COLS:(c + 1) * FFN_COLS], preferred_element_type=F32)
        up = jnp.dot(h, wi_ref[:, D_FF + c * FFN_COLS:D_FF + (c + 1) * FFN_COLS], preferred_element_type=F32)
        act_ref[:, c * FFN_COLS:(c + 1) * FFN_COLS] = (gate * jax.nn.sigmoid(gate) * up).astype(BF16)
    y = jnp.dot(act_ref[...], wo_ref[...], preferred_element_type=F32)
    o_ref[...] = x_ref[...] + 0.5 * _rms(y, post_g_ref[...])


def _ffn(x, pre_g, wi, wo, post_g):
    n = x.shape[0]
    rows = pl.BlockSpec((FFN_ROWS, D_MODEL), lambda i: (i, 0))
    return pl.pallas_call(
        _ffn_kernel,
        grid=(n // FFN_ROWS,),
        in_specs=[rows, _resident((1, D_MODEL)), _resident(wi.shape), _resident(wo.shape),
                  _resident((1, D_MODEL))],
        out_specs=rows,
        out_shape=jax.ShapeDtypeStruct((n, D_MODEL), F32),
        scratch_shapes=[pltpu.VMEM((FFN_ROWS, D_FF), BF16)],
        compiler_params=_params("parallel"),
        name="ffn",
    )(x, pre_g, wi, wo, post_g)


def _proj_kernel(x_ref, g_ref, wb0_ref, wb1_ref, wb2_ref, wc_ref,
                 qkvb0_ref, qkvb1_ref, qkvb2_ref, qkvc_ref, class_ref):
    h = _rms(x_ref[...], g_ref[...]).astype(BF16)
    groups = ((wb0_ref, qkvb0_ref), (wb1_ref, qkvb1_ref), (wb2_ref, qkvb2_ref))
    for (w_ref, out_ref), (_, dil) in zip(groups, B_PATTERNS):
        qkv = jnp.dot(h, w_ref[...], preferred_element_type=F32)
        if dil == 1:
            out_ref[...] = qkv.astype(BF16)
            continue
        width = qkv.shape[1]
        for c in range(width // LANES):
            class_ref[c] = qkv[:, c * LANES:(c + 1) * LANES]
        for r in range(dil):
            for c in range(width // LANES):
                lanes = slice(r * width + c * LANES, r * width + (c + 1) * LANES)
                out_ref[:, lanes] = class_ref[c, pl.ds(r, PROJ_ROWS // dil, stride=dil), :].astype(BF16)
    qkvc = jnp.dot(h, wc_ref[...], preferred_element_type=F32)
    qkvc_ref[:, :C_WIDTH] = (qkvc[:, :C_WIDTH] * C_Q_SCALE).astype(BF16)
    qkvc_ref[:, C_WIDTH:] = qkvc[:, C_WIDTH:].astype(BF16)


def _proj(x, g, wbs, wc):
    n = x.shape[0]
    rows = lambda w, dil=1: pl.BlockSpec((PROJ_ROWS // dil, dil * w), lambda i: (i, 0))
    weights = [*wbs, wc]
    dils = [*(dil for _, dil in B_PATTERNS), 1]
    return pl.pallas_call(
        _proj_kernel,
        grid=(n // PROJ_ROWS,),
        in_specs=[rows(D_MODEL), _resident((1, D_MODEL))] + [_resident(w.shape) for w in weights],
        out_specs=[rows(w.shape[1], dil) for w, dil in zip(weights, dils)],
        out_shape=[jax.ShapeDtypeStruct((n // dil, dil * w.shape[1]), BF16) for w, dil in zip(weights, dils)],
        scratch_shapes=[pltpu.VMEM((3 * B_GROUP_WIDTH // LANES, PROJ_ROWS, LANES), F32)],
        compiler_params=_params("parallel"),
        name="proj",
    )(x, g, *weights)


def _band_attention_kernel(qkv_ref, o_ref, lse_ref, *, dil):
    length = qkv_ref.shape[0]
    n_blocks = length // B_BLOCK
    window = min(2 * B_BLOCK, length)
    r_idx = lax.broadcasted_iota(jnp.int32, (B_BLOCK, window), 0)
    c_idx = lax.broadcasted_iota(jnp.int32, (B_BLOCK, window), 1)
    first_head = lax.broadcasted_iota(jnp.int32, (B_BLOCK, 2 * HEAD_DIM), 1) < HEAD_DIM
    contract_last = (((1,), (1,)), ((), ()))
    pair_width = 2 * HEAD_DIM
    n_pairs = B_GROUP_WIDTH // pair_width

    def blocks(tasks):
        qs, ks, vs, valids, dests = [], [], [], [], []
        for r, i in tasks:
            base = r * 3 * B_GROUP_WIDTH
            cur = pl.ds(pl.multiple_of(i * B_BLOCK, B_BLOCK), B_BLOCK)
            start = jnp.maximum(i - 1, 0) * B_BLOCK if n_blocks > 1 else 0
            keys = pl.ds(pl.multiple_of(start, B_BLOCK), window)
            ahead = r_idx + (i * B_BLOCK - start) - c_idx
            valid = jnp.logical_and(ahead >= 0, ahead <= B_BLOCK)
            for p in range(n_pairs):
                lanes = lambda part: slice(base + part * B_GROUP_WIDTH + p * pair_width,
                                           base + part * B_GROUP_WIDTH + (p + 1) * pair_width)
                q2 = qkv_ref[cur, lanes(0)] * SCALE
                zero = jnp.zeros_like(q2)
                qs += [jnp.where(first_head, q2, zero), jnp.where(first_head, zero, q2)]
                ks += [qkv_ref[keys, lanes(1)]] * 2
                vs += [qkv_ref[keys, lanes(2)]] * 2
                valids += [valid] * 2
                dests.append((cur, slice(r * B_GROUP_WIDTH + p * pair_width,
                                         r * B_GROUP_WIDTH + (p + 1) * pair_width)))
        scores = [lax.dot_general(q, k, contract_last, preferred_element_type=F32) for q, k in zip(qs, ks)]
        probs, stats = [], []
        for s, valid in zip(scores, valids):
            s = jnp.where(valid, s, -jnp.inf)
            m = jnp.max(s, axis=1, keepdims=True)
            e = jnp.exp(s - m)
            l = jnp.sum(e, axis=1, keepdims=True)
            probs.append(e.astype(BF16))
            stats.append((l, m + jnp.log(l)))
        outs = [jnp.dot(p, v, preferred_element_type=F32) / l for p, v, (l, _) in zip(probs, vs, stats)]
        for d, (rows, lanes) in enumerate(dests):
            o_ref[rows, lanes] = jnp.where(first_head, outs[2 * d], outs[2 * d + 1])
            lse_ref[rows, lanes] = jnp.where(first_head, stats[2 * d][1], stats[2 * d + 1][1])

    if n_blocks == 1:
        for r in range(0, dil, B_TASKS):
            blocks([(r + t, 0) for t in range(B_TASKS)])
    else:
        for r in range(dil):
            def body(step, carry, r=r):
                blocks([(r, step * B_TASKS + t) for t in range(B_TASKS)])
                return carry
            lax.fori_loop(0, n_blocks // B_TASKS, body, 0)


def _band_attention(qkvb, group, bsz):
    dil = B_PATTERNS[group][1]
    length = qkvb.shape[0] // bsz
    width = qkvb.shape[1] // dil
    view = qkvb.reshape(bsz, length, dil * width)
    out_spec = pl.BlockSpec((None, length, dil * B_GROUP_WIDTH), lambda b: (b, 0, 0))
    out_shape = jax.ShapeDtypeStruct((bsz, length, dil * B_GROUP_WIDTH), F32)
    o, lse = pl.pallas_call(
        functools.partial(_band_attention_kernel, dil=dil),
        grid=(bsz,),
        in_specs=[pl.BlockSpec((None, length, dil * width), lambda b: (b, 0, 0))],
        out_specs=[out_spec, out_spec],
        out_shape=[out_shape, out_shape],
        compiler_params=_params("parallel"),
        name=f"band_attention_g{group}",
    )(view)
    return o.reshape(bsz * length, dil * B_GROUP_WIDTH), lse.reshape(bsz * length, dil * B_GROUP_WIDTH)


def _stick_breaking_tiles(qs, ks, vs, upper, mask):
    contract_last = (((1,), (1,)), ((), ()))
    zs = [lax.dot_general(q, k, contract_last, preferred_element_type=F32) for q, k in zip(qs, ks)]
    log_keeps, logits, afters = [], [], []
    for nw in zs:
        log_keep = jnp.minimum(nw, 0.0) - jnp.log2(1.0 + jnp.exp2(-jnp.abs(nw)))
        if mask is not None:
            log_keep = jnp.where(mask, log_keep, 0.0)
        afters.append(jnp.dot(log_keep.astype(BF16), upper, preferred_element_type=F32))
        log_keeps.append(log_keep)
        logits.append(log_keep - nw)
    out = []
    for log_keep, logit, after, v in zip(log_keeps, logits, afters, vs):
        a = jnp.exp2(logit + after)
        if mask is not None:
            a = jnp.where(mask, a, 0.0)
        contrib = jnp.dot(a.astype(BF16), v, preferred_element_type=F32)
        out.append((contrib, jnp.sum(log_keep, axis=1, keepdims=True)))
    return out


def _stick_breaking_kernel(q_ref, k_ref, v_ref, o_ref, acc_ref, passed_ref):
    n_blocks = q_ref.shape[0] // C_BLOCK
    row = lax.broadcasted_iota(jnp.int32, (C_BLOCK, C_BLOCK), 0)
    col = lax.broadcasted_iota(jnp.int32, (C_BLOCK, C_BLOCK), 1)
    strictly_lower = col < row
    upper = jnp.where(row > col, 1.0, 0.0).astype(BF16)
    pair_width = 2 * HEAD_DIM
    pairs = [slice(p * pair_width, (p + 1) * pair_width) for p in range(C_HEADS_PER_STEP // 2)]
    first_head = lax.broadcasted_iota(jnp.int32, (C_BLOCK, pair_width), 1) < HEAD_DIM

    def q_block(i, carry):
        rows_i = pl.ds(pl.multiple_of(i * C_BLOCK, C_BLOCK), C_BLOCK)

        def tiles(rows_j, mask):
            qs, ks, vs = [], [], []
            for lanes in pairs:
                q2 = q_ref[rows_i, lanes]
                zero = jnp.zeros_like(q2)
                qs += [jnp.where(first_head, q2, zero), jnp.where(first_head, zero, q2)]
                ks += [k_ref[rows_j, lanes]] * 2
                vs += [v_ref[rows_j, lanes]] * 2
            res = _stick_breaking_tiles(qs, ks, vs, upper, mask)
            return [(jnp.where(first_head, res[2 * p][0], res[2 * p + 1][0]),
                     jnp.where(first_head, res[2 * p][1], res[2 * p + 1][1])) for p in range(len(pairs))]

        most_alive = None
        for p, (contrib, row_sum) in enumerate(tiles(rows_i, strictly_lower)):
            acc_ref[p] = contrib
            passed_ref[p] = row_sum
            most_alive = row_sum if most_alive is None else jnp.maximum(most_alive, row_sum)

        def k_block(state):
            step, _ = state
            rows_j = pl.ds(pl.multiple_of((i - 1 - step) * C_BLOCK, C_BLOCK), C_BLOCK)
            most_alive = None
            for p, (contrib, row_sum) in enumerate(tiles(rows_j, None)):
                passed = passed_ref[p]
                acc_ref[p] += contrib * jnp.exp2(passed)
                passed = passed + row_sum
                passed_ref[p] = passed
                most_alive = passed if most_alive is None else jnp.maximum(most_alive, passed)
            return step + 1, jnp.max(most_alive)

        def more_to_do(state):
            step, most_alive = state
            return jnp.logical_and(step < i, most_alive > C_DEAD_LOG2)

        lax.while_loop(more_to_do, k_block, (jnp.int32(0), jnp.max(most_alive)))
        for p, lanes in enumerate(pairs):
            o_ref[rows_i, lanes] = acc_ref[p].astype(o_ref.dtype)
        return carry

    lax.fori_loop(0, n_blocks, q_block, 0)


def _stick_breaking(qkvc):
    bsz, s, _ = qkvc.shape
    width = C_HEADS_PER_STEP * HEAD_DIM
    n_steps = C_WIDTH // width

    def spec(part):
        return pl.BlockSpec((None, s, width), lambda b, p: (b, 0, part * n_steps + p))

    return pl.pallas_call(
        _stick_breaking_kernel,
        grid=(bsz, n_steps),
        in_specs=[spec(0), spec(1), spec(2)],
        out_specs=pl.BlockSpec((None, s, width), lambda b, p: (b, 0, p)),
        out_shape=jax.ShapeDtypeStruct((bsz, s, C_WIDTH), BF16),
        scratch_shapes=[pltpu.VMEM((C_HEADS_PER_STEP // 2, C_BLOCK, 2 * HEAD_DIM), F32),
                        pltpu.VMEM((C_HEADS_PER_STEP // 2, C_BLOCK, 2 * HEAD_DIM), F32)],
        compiler_params=_params("parallel", "parallel"),
        name="stick_breaking",
    )(qkvc, qkvc, qkvc)


def _token_order(src_ref, scratch_ref, dil):
    if dil == 1:
        return src_ref[...]
    tiles, rows, _ = scratch_ref.shape
    for r in range(dil):
        for c in range(tiles):
            lanes = slice((r * tiles + c) * LANES, (r * tiles + c + 1) * LANES)
            scratch_ref[c, pl.ds(r, rows // dil, stride=dil), :] = src_ref[:, lanes]
    return jnp.concatenate([scratch_ref[c] for c in range(tiles)], axis=1)


def _merge_kernel(x_ref, o0_ref, l0_ref, o1_ref, l1_ref, o2_ref, l2_ref, yc_ref,
                  pre_g_ref, wa_ref, wg_ref, ln_g_ref, ln_b_ref, ws_ref, bs_ref,
                  wpa_ref, wpb_ref, wpc_ref, wo_ref, post_g_ref, out_ref, *order_refs):
    h = _rms(x_ref[...], pre_g_ref[...]).astype(BF16)
    za = jnp.dot(h, wa_ref[...], preferred_element_type=F32)
    gate_a = jnp.dot(h, wg_ref[:, :D_MODEL], preferred_element_type=F32)

    z = jax.nn.gelu(za, approximate=True)
    u = z[:, :A_WIDTH]
    v = z[:, A_WIDTH:]
    mu = jnp.mean(v, axis=-1, keepdims=True)
    vc = v - mu
    v = vc * lax.rsqrt(jnp.mean(vc * vc, axis=-1, keepdims=True) + EPS) * ln_g_ref[...] + ln_b_ref[...]
    v = v.astype(BF16)
    gate_b = jnp.dot(h, wg_ref[:, D_MODEL:2 * D_MODEL], preferred_element_type=F32)
    t_idx = lax.broadcasted_iota(jnp.int32, (A_CHUNK, A_CHUNK), 0)
    s_idx = lax.broadcasted_iota(jnp.int32, (A_CHUNK, A_CHUNK), 1)
    chunks = []
    for c in range(MERGE_ROWS // A_CHUNK):
        groups = []
        for g in range(A_GROUPS):
            w = jnp.where(s_idx <= t_idx, ws_ref[g], 0.0).astype(BF16)
            vg = v[c * A_CHUNK:(c + 1) * A_CHUNK, g * A_GROUP_DIM:(g + 1) * A_GROUP_DIM]
            groups.append(jnp.dot(w, vg, preferred_element_type=F32))
        chunks.append(jnp.concatenate(groups, axis=1) + bs_ref[...])
    y_a = u * jnp.concatenate(chunks, axis=0)

    dils = [dil for _, dil in B_PATTERNS]
    o0, o1, o2 = [_token_order(src, order_refs[2 * g], dils[g]) for g, src in enumerate((o0_ref, o1_ref, o2_ref))]
    l0, l1, l2 = [_token_order(src, order_refs[2 * g + 1], dils[g]) for g, src in enumerate((l0_ref, l1_ref, l2_ref))]
    m = jnp.maximum(jnp.maximum(l0, l1), l2)
    w0, w1, w2 = jnp.exp(l0 - m), jnp.exp(l1 - m), jnp.exp(l2 - m)
    y_b = (w0 * o0 + w1 * o1 + w2 * o2) / (w0 + w1 + w2)

    p_c = jnp.dot(yc_ref[...], wpc_ref[...], preferred_element_type=F32)
    p_b = jnp.dot(y_b.astype(BF16), wpb_ref[...], preferred_element_type=F32)
    gate_c = jnp.dot(h, wg_ref[:, 2 * D_MODEL:], preferred_element_type=F32)
    p_a = jnp.dot(y_a.astype(BF16), wpa_ref[...], preferred_element_type=F32)
    merged = (jax.nn.sigmoid(gate_a) * p_a + jax.nn.sigmoid(gate_b) * p_b + jax.nn.sigmoid(gate_c) * p_c)
    y = jnp.dot(merged.astype(BF16), wo_ref[...], preferred_element_type=F32)
    out_ref[...] = x_ref[...] + _rms(y, post_g_ref[...])


def _merge(x, b_parts, yc, *params):
    n = x.shape[0]
    rows = lambda w: pl.BlockSpec((MERGE_ROWS, w), lambda i: (i, 0))
    b_flat = [a for part in b_parts for a in part]
    b_specs = [pl.BlockSpec((MERGE_ROWS // dil, dil * B_GROUP_WIDTH), lambda i: (i, 0))
               for _, dil in B_PATTERNS for _ in range(2)]
    return pl.pallas_call(
        _merge_kernel,
        grid=(n // MERGE_ROWS,),
        in_specs=[rows(D_MODEL)] + b_specs + [rows(C_WIDTH)] + [_resident(p.shape) for p in params],
        out_specs=rows(D_MODEL),
        out_shape=jax.ShapeDtypeStruct((n, D_MODEL), F32),
        scratch_shapes=[pltpu.VMEM((B_GROUP_WIDTH // LANES, MERGE_ROWS, LANES), F32)] * (2 * B_GROUPS),
        compiler_params=_params("parallel"),
        name="merge",
    )(x, *b_flat, yc, *params)


def kernel(x, ffn1_pre_g, ffn1_wi, ffn1_wo, ffn1_post_g, mix_pre_g, w_in, a_ln_g, a_ln_b, a_ws, a_bs,
           w_pa, w_pb, w_pc, w_o, mix_post_g, ffn2_pre_g, ffn2_wi, ffn2_wo, ffn2_post_g):
    bsz, s, d = x.shape
    n = bsz * s
    depth = ffn1_wi.shape[0]
    split_a = 2 * A_WIDTH
    split_b = split_a + 3 * B_WIDTH
    split_c = split_b + 3 * C_WIDTH
    row = lambda p: p.reshape(1, -1)

    x = x.reshape(n, d)
    for l in range(depth):
        x = _ffn(x, row(ffn1_pre_g[l]), ffn1_wi[l].astype(BF16), ffn1_wo[l].astype(BF16),
                 row(ffn1_post_g[l]))

        w = w_in[l].astype(BF16)
        wb = w[:, split_a:split_b].reshape(d, 3, B_GROUPS, B_GROUP_WIDTH)
        wbs = [wb[:, :, g].reshape(d, 3 * B_GROUP_WIDTH) for g in range(B_GROUPS)]
        *qkvbs, qkvc = _proj(x, row(mix_pre_g[l]), wbs, w[:, split_b:split_c])
        b_parts = [_band_attention(qkvb, g, bsz) for g, qkvb in enumerate(qkvbs)]
        yc = _stick_breaking(qkvc.reshape(bsz, s, 3 * C_WIDTH)).reshape(n, C_WIDTH)

        bs_full = jnp.repeat(a_bs[l].T, A_GROUP_DIM, axis=1)
        x = _merge(x, b_parts, yc, row(mix_pre_g[l]), w[:, :split_a], w[:, split_c:],
                   row(a_ln_g[l]), row(a_ln_b[l]), a_ws[l], bs_full,
                   w_pa[l].astype(BF16), w_pb[l].astype(BF16), w_pc[l].astype(BF16), w_o[l].astype(BF16),
                   row(mix_post_g[l]))

        x = _ffn(x, row(ffn2_pre_g[l]), ffn2_wi[l].astype(BF16), ffn2_wo[l].astype(BF16),
                 row(ffn2_post_g[l]))
    return x.reshape(bsz, s, d)
```

```python
import functools

import jax
import jax.numpy as jnp
from jax import lax
from jax.experimental import pallas as pl
from jax.experimental.pallas import tpu as pltpu

D_MODEL = 1024
HEAD_DIM = 64
A_GROUPS = 4
A_CHUNK = 128
A_GROUP_DIM = 128
A_WIDTH = A_GROUPS * A_GROUP_DIM
B_PATTERNS = ((128, 1), (512, 4), (2048, 16))
B_GROUPS = len(B_PATTERNS)
B_HEADS_PER_GROUP = 4
B_GROUP_WIDTH = B_HEADS_PER_GROUP * HEAD_DIM
B_WIDTH = B_GROUPS * B_GROUP_WIDTH
B_BLOCK = 128
B_TASKS = 2
C_HEADS = 8
C_WIDTH = C_HEADS * HEAD_DIM
C_BLOCK = 256
C_HEADS_PER_STEP = 4
SPLIT_A = 2 * A_WIDTH
SPLIT_B = SPLIT_A + 3 * B_WIDTH
SPLIT_C = SPLIT_B + 3 * C_WIDTH
D_FF = 2816
EPS = 1e-6
SCALE = HEAD_DIM ** -0.5
LOG2E = 1.4426950408889634
C_Q_SCALE = -SCALE * LOG2E
C_DEAD_LOG2 = -160.0

F32 = jnp.float32
BF16 = jnp.bfloat16

LANES = 128

VMEM_LIMIT_BYTES = 56 * 1024 * 1024

FFN_ROWS = 1024
FFN_COLS = 256
PROJ_ROWS = 512
MERGE_ROWS = 512
MERGE_SUBTILES = 2


def _params(*semantics):
    return pltpu.CompilerParams(dimension_semantics=semantics, vmem_limit_bytes=VMEM_LIMIT_BYTES)


def _rms(x, g):
    return x * lax.rsqrt(jnp.mean(x * x, axis=-1, keepdims=True) + EPS) * g


def _sigmoid(x):
    return 0.5 * jnp.tanh(0.5 * x) + 0.5


def _resident(stacked, layer):
    zeros = (0,) * (stacked.ndim - 1)
    return pl.BlockSpec((None, *stacked.shape[1:]), lambda *_: (layer, *zeros), pipeline_mode=pl.Buffered(1))


def _ffn_kernel(x_ref, pre_g_ref, wi_ref, wo_ref, post_g_ref, o_ref, act_ref):
    h = _rms(x_ref[...], pre_g_ref[...]).astype(BF16)
    for c in range(D_FF // FFN_COLS):
        gate = jnp.dot(h, wi_ref[:, c * FFN_COLS:(c + 1) * FFN_COLS], preferred_element_type=F32)
        up = jnp.dot(h, wi_ref[:, D_FF + c * FFN_COLS:D_FF + (c + 1) * FFN_COLS], preferred_element_type=F32)
        act_ref[:, c * FFN_COLS:(c + 1) * FFN_COLS] = (gate * jax.nn.sigmoid(gate) * up).astype(BF16)
    y = jnp.dot(act_ref[...], wo_ref[...], preferred_element_type=F32)
    o_ref[...] = x_ref[...] + 0.5 * _rms(y, post_g_ref[...])


def _ffn(x, layer, pre_g, wi, wo, post_g):
    n = x.shape[0]
    rows = pl.BlockSpec((FFN_ROWS, D_MODEL), lambda i: (i, 0))
    return pl.pallas_call(
        _ffn_kernel,
        grid=(n // FFN_ROWS,),
        in_specs=[rows] + [_resident(p, layer) for p in (pre_g, wi, wo, post_g)],
        out_specs=rows,
        out_shape=jax.ShapeDtypeStruct((n, D_MODEL), F32),
        scratch_shapes=[pltpu.VMEM((FFN_ROWS, D_FF), BF16)],
        compiler_params=_params("parallel"),
        name="ffn",
    )(x, pre_g, wi, wo, post_g)


def _proj_kernel(x_ref, g_ref, w_in_ref, qkvb0_ref, qkvb1_ref, qkvb2_ref, qkvc_ref, class_ref):
    h = _rms(x_ref[...], g_ref[...]).astype(BF16)
    tiles = B_GROUP_WIDTH // LANES
    for g, (out_ref, (_, dil)) in enumerate(zip((qkvb0_ref, qkvb1_ref, qkvb2_ref), B_PATTERNS)):
        for part in range(3):
            first = SPLIT_A + part * B_WIDTH + g * B_GROUP_WIDTH
            y = jnp.dot(h, w_in_ref[:, first:first + B_GROUP_WIDTH], preferred_element_type=F32)
            if dil == 1:
                out_ref[:, part * B_GROUP_WIDTH:(part + 1) * B_GROUP_WIDTH] = y.astype(BF16)
            else:
                for c in range(tiles):
                    class_ref[part * tiles + c] = y[:, c * LANES:(c + 1) * LANES]
        if dil > 1:
            width = 3 * B_GROUP_WIDTH
            for r in range(dil):
                for c in range(width // LANES):
                    lanes = slice(r * width + c * LANES, r * width + (c + 1) * LANES)
                    out_ref[:, lanes] = class_ref[c, pl.ds(r, PROJ_ROWS // dil, stride=dil), :].astype(BF16)
    qkvc = jnp.dot(h, w_in_ref[:, SPLIT_B:SPLIT_C], preferred_element_type=F32)
    qkvc_ref[:, :C_WIDTH] = (qkvc[:, :C_WIDTH] * C_Q_SCALE).astype(BF16)
    qkvc_ref[:, C_WIDTH:] = qkvc[:, C_WIDTH:].astype(BF16)


def _proj(x, layer, g, w_in):
    n = x.shape[0]
    rows = lambda w, dil=1: pl.BlockSpec((PROJ_ROWS // dil, dil * w), lambda i: (i, 0))
    widths = [3 * B_GROUP_WIDTH] * B_GROUPS + [3 * C_WIDTH]
    dils = [*(dil for _, dil in B_PATTERNS), 1]
    return pl.pallas_call(
        _proj_kernel,
        grid=(n // PROJ_ROWS,),
        in_specs=[rows(D_MODEL), _resident(g, layer), _resident(w_in, layer)],
        out_specs=[rows(w, dil) for w, dil in zip(widths, dils)],
        out_shape=[jax.ShapeDtypeStruct((n // dil, dil * w), BF16) for w, dil in zip(widths, dils)],
        scratch_shapes=[pltpu.VMEM((3 * B_GROUP_WIDTH // LANES, PROJ_ROWS, LANES), F32)],
        compiler_params=_params("parallel"),
        name="proj",
    )(x, g, w_in)


def _band_attention_kernel(qkv_ref, o_ref, lse_ref, *, dil):
    length = qkv_ref.shape[0]
    n_blocks = length // B_BLOCK
    window = min(2 * B_BLOCK, length)
    r_idx = lax.broadcasted_iota(jnp.int32, (B_BLOCK, window), 0)
    c_idx = lax.broadcasted_iota(jnp.int32, (B_BLOCK, window), 1)
    first_head = lax.broadcasted_iota(jnp.int32, (B_BLOCK, 2 * HEAD_DIM), 1) < HEAD_DIM
    contract_last = (((1,), (1,)), ((), ()))
    pair_width = 2 * HEAD_DIM
    n_pairs = B_GROUP_WIDTH // pair_width

    def blocks(tasks):
        qs, ks, vs, valids, dests = [], [], [], [], []
        for r, i in tasks:
            base = r * 3 * B_GROUP_WIDTH
            cur = pl.ds(pl.multiple_of(i * B_BLOCK, B_BLOCK), B_BLOCK)
            start = jnp.maximum(i - 1, 0) * B_BLOCK if n_blocks > 1 else 0
            keys = pl.ds(pl.multiple_of(start, B_BLOCK), window)
            ahead = r_idx + (i * B_BLOCK - start) - c_idx
            valid = jnp.logical_and(ahead >= 0, ahead <= B_BLOCK)
            for p in range(n_pairs):
                lanes = lambda part: slice(base + part * B_GROUP_WIDTH + p * pair_width,
                                           base + part * B_GROUP_WIDTH + (p + 1) * pair_width)
                q2 = qkv_ref[cur, lanes(0)] * SCALE
                zero = jnp.zeros_like(q2)
                qs += [jnp.where(first_head, q2, zero), jnp.where(first_head, zero, q2)]
                ks += [qkv_ref[keys, lanes(1)]] * 2
                vs += [qkv_ref[keys, lanes(2)]] * 2
                valids += [valid] * 2
                dests.append((cur, slice(r * B_GROUP_WIDTH + p * pair_width,
                                         r * B_GROUP_WIDTH + (p + 1) * pair_width)))
        scores = [lax.dot_general(q, k, contract_last, preferred_element_type=F32) for q, k in zip(qs, ks)]
        probs, stats = [], []
        for s, valid in zip(scores, valids):
            s = jnp.where(valid, s, -jnp.inf)
            m = jnp.max(s, axis=1, keepdims=True)
            e = jnp.exp(s - m)
            l = jnp.sum(e, axis=1, keepdims=True)
            probs.append(e.astype(BF16))
            stats.append((l, m + jnp.log(l)))
        outs = [jnp.dot(p, v, preferred_element_type=F32) / l for p, v, (l, _) in zip(probs, vs, stats)]
        for d, (rows, lanes) in enumerate(dests):
            o_ref[rows, lanes] = jnp.where(first_head, outs[2 * d], outs[2 * d + 1])
            lse_ref[rows, lanes] = jnp.where(first_head, stats[2 * d][1], stats[2 * d + 1][1])

    if n_blocks == 1:
        for r in range(0, dil, B_TASKS):
            blocks([(r + t, 0) for t in range(B_TASKS)])
    else:
        for r in range(dil):
            def body(step, carry, r=r):
                blocks([(r, step * B_TASKS + t) for t in range(B_TASKS)])
                return carry
            lax.fori_loop(0, n_blocks // B_TASKS, body, 0)


def _band_attention(qkvb, group, bsz):
    dil = B_PATTERNS[group][1]
    length = qkvb.shape[0] // bsz
    width = qkvb.shape[1] // dil
    view = qkvb.reshape(bsz, length, dil * width)
    out_spec = pl.BlockSpec((None, length, dil * B_GROUP_WIDTH), lambda b: (b, 0, 0))
    out_shape = jax.ShapeDtypeStruct((bsz, length, dil * B_GROUP_WIDTH), F32)
    o, lse = pl.pallas_call(
        functools.partial(_band_attention_kernel, dil=dil),
        grid=(bsz,),
        in_specs=[pl.BlockSpec((None, length, dil * width), lambda b: (b, 0, 0))],
        out_specs=[out_spec, out_spec],
        out_shape=[out_shape, out_shape],
        compiler_params=_params("parallel"),
        name=f"band_attention_g{group}",
    )(view)
    return o.reshape(bsz * length, dil * B_GROUP_WIDTH), lse.reshape(bsz * length, dil * B_GROUP_WIDTH)


def _stick_breaking_tiles(qs, ks, vs, upper, mask):
    contract_last = (((1,), (1,)), ((), ()))
    zs = [lax.dot_general(q, k, contract_last, preferred_element_type=F32) for q, k in zip(qs, ks)]
    log_keeps, logits, afters = [], [], []
    for nw in zs:
        log_keep = jnp.minimum(nw, 0.0) - jnp.log2(1.0 + jnp.exp2(-jnp.abs(nw)))
        if mask is not None:
            log_keep = jnp.where(mask, log_keep, 0.0)
        afters.append(jnp.dot(log_keep.astype(BF16), upper, preferred_element_type=F32))
        log_keeps.append(log_keep)
        logits.append(log_keep - nw)
    out = []
    for log_keep, logit, after, v in zip(log_keeps, logits, afters, vs):
        a = jnp.exp2(logit + after)
        if mask is not None:
            a = jnp.where(mask, a, 0.0)
        contrib = jnp.dot(a.astype(BF16), v, preferred_element_type=F32)
        out.append((contrib, jnp.sum(log_keep, axis=1, keepdims=True)))
    return out


def _stick_breaking_kernel(q_ref, k_ref, v_ref, o_ref, acc_ref, passed_ref):
    n_blocks = q_ref.shape[0] // C_BLOCK
    row = lax.broadcasted_iota(jnp.int32, (C_BLOCK, C_BLOCK), 0)
    col = lax.broadcasted_iota(jnp.int32, (C_BLOCK, C_BLOCK), 1)
    strictly_lower = col < row
    upper = jnp.where(row > col, 1.0, 0.0).astype(BF16)
    pair_width = 2 * HEAD_DIM
    pairs = [slice(p * pair_width, (p + 1) * pair_width) for p in range(C_HEADS_PER_STEP // 2)]
    first_head = lax.broadcasted_iota(jnp.int32, (C_BLOCK, pair_width), 1) < HEAD_DIM

    def q_block(i, carry):
        rows_i = pl.ds(pl.multiple_of(i * C_BLOCK, C_BLOCK), C_BLOCK)

        def tiles(rows_j, mask):
            qs, ks, vs = [], [], []
            for lanes in pairs:
                q2 = q_ref[rows_i, lanes]
                zero = jnp.zeros_like(q2)
                qs += [jnp.where(first_head, q2, zero), jnp.where(first_head, zero, q2)]
                ks += [k_ref[rows_j, lanes]] * 2
                vs += [v_ref[rows_j, lanes]] * 2
            res = _stick_breaking_tiles(qs, ks, vs, upper, mask)
            return [(jnp.where(first_head, res[2 * p][0], res[2 * p + 1][0]),
                     jnp.where(first_head, res[2 * p][1], res[2 * p + 1][1])) for p in range(len(pairs))]

        most_alive = None
        for p, (contrib, row_sum) in enumerate(tiles(rows_i, strictly_lower)):
            acc_ref[p] = contrib
            passed_ref[p] = row_sum
            most_alive = row_sum if most_alive is None else jnp.maximum(most_alive, row_sum)

        def k_block(state):
            step, _ = state
            rows_j = pl.ds(pl.multiple_of((i - 1 - step) * C_BLOCK, C_BLOCK), C_BLOCK)
            most_alive = None
            for p, (contrib, row_sum) in enumerate(tiles(rows_j, None)):
                passed = passed_ref[p]
                acc_ref[p] += contrib * jnp.exp2(passed)
                passed = passed + row_sum
                passed_ref[p] = passed
                most_alive = passed if most_alive is None else jnp.maximum(most_alive, passed)
            return step + 1, jnp.max(most_alive)

        def more_to_do(state):
            step, most_alive = state
            return jnp.logical_and(step < i, most_alive > C_DEAD_LOG2)

        lax.while_loop(more_to_do, k_block, (jnp.int32(0), jnp.max(most_alive)))
        for p, lanes in enumerate(pairs):
            o_ref[rows_i, lanes] = acc_ref[p].astype(o_ref.dtype)
        return carry

    lax.fori_loop(0, n_blocks, q_block, 0)


def _stick_breaking(qkvc):
    bsz, s, _ = qkvc.shape
    width = C_HEADS_PER_STEP * HEAD_DIM
    n_steps = C_WIDTH // width

    def spec(part):
        return pl.BlockSpec((None, s, width), lambda b, p: (b, 0, part * n_steps + p))

    return pl.pallas_call(
        _stick_breaking_kernel,
        grid=(bsz, n_steps),
        in_specs=[spec(0), spec(1), spec(2)],
        out_specs=pl.BlockSpec((None, s, width), lambda b, p: (b, 0, p)),
        out_shape=jax.ShapeDtypeStruct((bsz, s, C_WIDTH), BF16),
        scratch_shapes=[pltpu.VMEM((C_HEADS_PER_STEP // 2, C_BLOCK, 2 * HEAD_DIM), F32),
                        pltpu.VMEM((C_HEADS_PER_STEP // 2, C_BLOCK, 2 * HEAD_DIM), F32)],
        compiler_params=_params("parallel", "parallel"),
        name="stick_breaking",
    )(qkvc, qkvc, qkvc)


def _fill_token_order(src_ref, scratch_ref, dil):
    tiles, rows, _ = scratch_ref.shape
    for r in range(dil):
        for c in range(tiles):
            lanes = slice((r * tiles + c) * LANES, (r * tiles + c + 1) * LANES)
            scratch_ref[c, pl.ds(r, rows // dil, stride=dil), :] = src_ref[:, lanes]


def _merge_kernel(x_ref, o0_ref, l0_ref, o1_ref, l1_ref, o2_ref, l2_ref, yc_ref,
                  pre_g_ref, w_in_ref, ln_g_ref, ln_b_ref, ws_ref, bs_ref,
                  wpa_ref, wpb_ref, wpc_ref, wo_ref, post_g_ref, out_ref, *order_refs):
    sub = MERGE_ROWS // MERGE_SUBTILES
    ranges = [slice(i * sub, (i + 1) * sub) for i in range(MERGE_SUBTILES)]
    dot = functools.partial(jnp.dot, preferred_element_type=F32)

    b_srcs = (o0_ref, l0_ref, o1_ref, l1_ref, o2_ref, l2_ref)
    b_dils = [dil for _, dil in B_PATTERNS for _ in range(2)]
    for src, scratch, dil in zip(b_srcs, order_refs, b_dils):
        if dil > 1:
            _fill_token_order(src, scratch, dil)

    def b_part(k, rows):
        if b_dils[k] == 1:
            return b_srcs[k][rows, :]
        return jnp.concatenate([order_refs[k][c, rows, :] for c in range(order_refs[k].shape[0])], axis=1)

    hs = [_rms(x_ref[rows, :], pre_g_ref[...]).astype(BF16) for rows in ranges]
    gate = lambda h, b: dot(h, w_in_ref[:, SPLIT_C + b * D_MODEL:SPLIT_C + (b + 1) * D_MODEL])
    zas = [dot(h, w_in_ref[:, :SPLIT_A]) for h in hs]
    gates_a = [gate(h, 0) for h in hs]

    us, vs = [], []
    for za in zas:
        z = jax.nn.gelu(za, approximate=True)
        v = z[:, A_WIDTH:]
        vc = v - jnp.mean(v, axis=-1, keepdims=True)
        v = vc * lax.rsqrt(jnp.mean(vc * vc, axis=-1, keepdims=True) + EPS) * ln_g_ref[...] + ln_b_ref[...]
        us.append(z[:, :A_WIDTH])
        vs.append(v.astype(BF16))
    gates_b = [gate(h, 1) for h in hs]
    t_idx = lax.broadcasted_iota(jnp.int32, (A_CHUNK, A_CHUNK), 0)
    s_idx = lax.broadcasted_iota(jnp.int32, (A_CHUNK, A_CHUNK), 1)
    ws = [jnp.where(s_idx <= t_idx, ws_ref[g], 0.0).astype(BF16) for g in range(A_GROUPS)]
    n_chunks = sub // A_CHUNK
    y_as = []
    for u, v in zip(us, vs):
        mixed = []
        for g in range(A_GROUPS):
            vg = jnp.concatenate([v[c * A_CHUNK:(c + 1) * A_CHUNK, g * A_GROUP_DIM:(g + 1) * A_GROUP_DIM]
                                  for c in range(n_chunks)], axis=1)
            mixed.append(dot(ws[g], vg))
        sv = jnp.concatenate(
            [jnp.concatenate([m[:, c * A_GROUP_DIM:(c + 1) * A_GROUP_DIM] for m in mixed], axis=1) + bs_ref[...]
             for c in range(n_chunks)], axis=0)
        y_as.append((u * sv).astype(BF16))

    y_bs = []
    for rows in ranges:
        o0, l0, o1, l1, o2, l2 = [b_part(k, rows) for k in range(6)]
        m = jnp.maximum(jnp.maximum(l0, l1), l2)
        w0, w1, w2 = jnp.exp(l0 - m), jnp.exp(l1 - m), jnp.exp(l2 - m)
        y_bs.append(((w0 * o0 + w1 * o1 + w2 * o2) / (w0 + w1 + w2)).astype(BF16))

    p_cs = [dot(yc_ref[rows, :], wpc_ref[...]) for rows in ranges]
    p_bs = [dot(y_b, wpb_ref[...]) for y_b in y_bs]
    gates_c = [gate(h, 2) for h in hs]
    p_as = [dot(y_a, wpa_ref[...]) for y_a in y_as]
    mergeds = [(_sigmoid(g_a) * p_a + _sigmoid(g_b) * p_b + _sigmoid(g_c) * p_c).astype(BF16)
               for g_a, p_a, g_b, p_b, g_c, p_c in zip(gates_a, p_as, gates_b, p_bs, gates_c, p_cs)]
    ys = [dot(merged, wo_ref[...]) for merged in mergeds]
    for rows, y in zip(ranges, ys):
        out_ref[rows, :] = x_ref[rows, :] + _rms(y, post_g_ref[...])


def _merge(x, b_parts, yc, layer, *params):
    n = x.shape[0]
    rows = lambda w: pl.BlockSpec((MERGE_ROWS, w), lambda i: (i, 0))
    b_flat = [a for part in b_parts for a in part]
    b_specs = [pl.BlockSpec((MERGE_ROWS // dil, dil * B_GROUP_WIDTH), lambda i: (i, 0))
               for _, dil in B_PATTERNS for _ in range(2)]
    return pl.pallas_call(
        _merge_kernel,
        grid=(n // MERGE_ROWS,),
        in_specs=[rows(D_MODEL)] + b_specs + [rows(C_WIDTH)] + [_resident(p, layer) for p in params],
        out_specs=rows(D_MODEL),
        out_shape=jax.ShapeDtypeStruct((n, D_MODEL), F32),
        scratch_shapes=[pltpu.VMEM((B_GROUP_WIDTH // LANES, MERGE_ROWS, LANES), F32)] * (2 * B_GROUPS),
        compiler_params=_params("parallel"),
        name="merge",
    )(x, *b_flat, yc, *params)


def kernel(x, ffn1_pre_g, ffn1_wi, ffn1_wo, ffn1_post_g, mix_pre_g, w_in, a_ln_g, a_ln_b, a_ws, a_bs,
           w_pa, w_pb, w_pc, w_o, mix_post_g, ffn2_pre_g, ffn2_wi, ffn2_wo, ffn2_post_g):
    bsz, s, d = x.shape
    n = bsz * s
    depth = ffn1_wi.shape[0]
    rows = lambda p: p[:, None, :]
    bf16 = lambda w: w.astype(BF16)
    ffn1 = (rows(ffn1_pre_g), bf16(ffn1_wi), bf16(ffn1_wo), rows(ffn1_post_g))
    ffn2 = (rows(ffn2_pre_g), bf16(ffn2_wi), bf16(ffn2_wo), rows(ffn2_post_g))
    w_in = bf16(w_in)
    mix_pre_g = rows(mix_pre_g)
    bs_full = jnp.repeat(jnp.swapaxes(a_bs, 1, 2), A_GROUP_DIM, axis=2)
    mix = (mix_pre_g, w_in, rows(a_ln_g), rows(a_ln_b), a_ws, bs_full,
           bf16(w_pa), bf16(w_pb), bf16(w_pc), bf16(w_o), rows(mix_post_g))

    x = x.reshape(n, d)
    for l in range(depth):
        x = _ffn(x, l, *ffn1)
        *qkvbs, qkvc = _proj(x, l, mix_pre_g, w_in)
        b_parts = [_band_attention(qkvb, g, bsz) for g, qkvb in enumerate(qkvbs)]
        yc = _stick_breaking(qkvc.reshape(bsz, s, 3 * C_WIDTH)).reshape(n, C_WIDTH)
        x = _merge(x, b_parts, yc, l, *mix)
        x = _ffn(x, l, *ffn2)
    return x.reshape(bsz, s, d)
```

```python
import functools

import jax
import jax.numpy as jnp
from jax import lax
from jax.experimental import pallas as pl
from jax.experimental.pallas import tpu as pltpu

D_MODEL = 1024
HEAD_DIM = 64
A_GROUPS = 4
A_CHUNK = 128
A_GROUP_DIM = 128
A_WIDTH = A_GROUPS * A_GROUP_DIM
B_PATTERNS = ((128, 1), (512, 4), (2048, 16))
B_GROUPS = len(B_PATTERNS)
B_HEADS_PER_GROUP = 4
B_GROUP_WIDTH = B_HEADS_PER_GROUP * HEAD_DIM
B_WIDTH = B_GROUPS * B_GROUP_WIDTH
B_BLOCK = 128
B_TASKS = 2
C_HEADS = 8
C_WIDTH = C_HEADS * HEAD_DIM
C_BLOCK = 256
C_HEADS_PER_STEP = 8
SPLIT_A = 2 * A_WIDTH
SPLIT_B = SPLIT_A + 3 * B_WIDTH
SPLIT_C = SPLIT_B + 3 * C_WIDTH
D_FF = 2816
EPS = 1e-6
SCALE = HEAD_DIM ** -0.5
LOG2E = 1.4426950408889634
C_Q_SCALE = -SCALE * LOG2E
C_DEAD_LOG2 = -160.0

F32 = jnp.float32
BF16 = jnp.bfloat16

LANES = 128

VMEM_LIMIT_BYTES = 56 * 1024 * 1024

FFN_ROWS = 1024
FFN_COLS = 256
PROJ_ROWS = 512
MERGE_ROWS = 512
MERGE_SUBTILES = 2


def _params(*semantics):
    return pltpu.CompilerParams(dimension_semantics=semantics, vmem_limit_bytes=VMEM_LIMIT_BYTES)


def _rms(x, g):
    return x * lax.rsqrt(jnp.mean(x * x, axis=-1, keepdims=True) + EPS) * g


def _sigmoid(x):
    return 0.5 * jnp.tanh(0.5 * x) + 0.5


def _resident(stacked, layer):
    zeros = (0,) * (stacked.ndim - 1)
    return pl.BlockSpec((None, *stacked.shape[1:]), lambda *_: (layer, *zeros), pipeline_mode=pl.Buffered(1))


def _ffn_kernel(x_ref, pre_g_ref, wi_ref, wo_ref, post_g_ref, o_ref, act_ref):
    h = _rms(x_ref[...], pre_g_ref[...]).astype(BF16)
    for c in range(D_FF // FFN_COLS):
        gate = jnp.dot(h, wi_ref[:, c * FFN_COLS:(c + 1) * FFN_COLS], preferred_element_type=F32)
        up = jnp.dot(h, wi_ref[:, D_FF + c * FFN_COLS:D_FF + (c + 1) * FFN_COLS], preferred_element_type=F32)
        act_ref[:, c * FFN_COLS:(c + 1) * FFN_COLS] = (gate * jax.nn.sigmoid(gate) * up).astype(BF16)
    y = jnp.dot(act_ref[...], wo_ref[...], preferred_element_type=F32)
    o_ref[...] = x_ref[...] + 0.5 * _rms(y, post_g_ref[...])


def _ffn(x, layer, pre_g, wi, wo, post_g):
    n = x.shape[0]
    rows = pl.BlockSpec((FFN_ROWS, D_MODEL), lambda i: (i, 0))
    return pl.pallas_call(
        _ffn_kernel,
        grid=(n // FFN_ROWS,),
        in_specs=[rows] + [_resident(p, layer) for p in (pre_g, wi, wo, post_g)],
        out_specs=rows,
        out_shape=jax.ShapeDtypeStruct((n, D_MODEL), F32),
        scratch_shapes=[pltpu.VMEM((FFN_ROWS, D_FF), BF16)],
        compiler_params=_params("parallel"),
        name="ffn",
    )(x, pre_g, wi, wo, post_g)


def _proj_kernel(x_ref, g_ref, w_in_ref, qkvb0_ref, qkvb1_ref, qkvb2_ref, qkvc_ref, class_ref):
    h = _rms(x_ref[...], g_ref[...]).astype(BF16)
    tiles = B_GROUP_WIDTH // LANES
    for g, (out_ref, (_, dil)) in enumerate(zip((qkvb0_ref, qkvb1_ref, qkvb2_ref), B_PATTERNS)):
        for part in range(3):
            first = SPLIT_A + part * B_WIDTH + g * B_GROUP_WIDTH
            y = jnp.dot(h, w_in_ref[:, first:first + B_GROUP_WIDTH], preferred_element_type=F32)
            if dil == 1:
                out_ref[:, part * B_GROUP_WIDTH:(part + 1) * B_GROUP_WIDTH] = y.astype(BF16)
            else:
                for c in range(tiles):
                    class_ref[part * tiles + c] = y[:, c * LANES:(c + 1) * LANES]
        if dil > 1:
            width = 3 * B_GROUP_WIDTH
            for r in range(dil):
                for c in range(width // LANES):
                    lanes = slice(r * width + c * LANES, r * width + (c + 1) * LANES)
                    out_ref[:, lanes] = class_ref[c, pl.ds(r, PROJ_ROWS // dil, stride=dil), :].astype(BF16)
    qkvc = jnp.dot(h, w_in_ref[:, SPLIT_B:SPLIT_C], preferred_element_type=F32)
    qkvc_ref[:, :C_WIDTH] = (qkvc[:, :C_WIDTH] * C_Q_SCALE).astype(BF16)
    qkvc_ref[:, C_WIDTH:] = qkvc[:, C_WIDTH:].astype(BF16)


def _proj(x, layer, g, w_in):
    n = x.shape[0]
    rows = lambda w, dil=1: pl.BlockSpec((PROJ_ROWS // dil, dil * w), lambda i: (i, 0))
    widths = [3 * B_GROUP_WIDTH] * B_GROUPS + [3 * C_WIDTH]
    dils = [*(dil for _, dil in B_PATTERNS), 1]
    return pl.pallas_call(
        _proj_kernel,
        grid=(n // PROJ_ROWS,),
        in_specs=[rows(D_MODEL), _resident(g, layer), _resident(w_in, layer)],
        out_specs=[rows(w, dil) for w, dil in zip(widths, dils)],
        out_shape=[jax.ShapeDtypeStruct((n // dil, dil * w), BF16) for w, dil in zip(widths, dils)],
        scratch_shapes=[pltpu.VMEM((3 * B_GROUP_WIDTH // LANES, PROJ_ROWS, LANES), F32)],
        compiler_params=_params("parallel"),
        name="proj",
    )(x, g, w_in)


def _band_attention_kernel(qkv_ref, o_ref, lse_ref, *, dil):
    length = qkv_ref.shape[0]
    n_blocks = length // B_BLOCK
    window = min(2 * B_BLOCK, length)
    r_idx = lax.broadcasted_iota(jnp.int32, (B_BLOCK, window), 0)
    c_idx = lax.broadcasted_iota(jnp.int32, (B_BLOCK, window), 1)
    first_head = lax.broadcasted_iota(jnp.int32, (B_BLOCK, 2 * HEAD_DIM), 1) < HEAD_DIM
    contract_last = (((1,), (1,)), ((), ()))
    pair_width = 2 * HEAD_DIM
    n_pairs = B_GROUP_WIDTH // pair_width

    def blocks(tasks):
        qs, ks, vs, valids, dests = [], [], [], [], []
        for r, i in tasks:
            base = r * 3 * B_GROUP_WIDTH
            cur = pl.ds(pl.multiple_of(i * B_BLOCK, B_BLOCK), B_BLOCK)
            start = jnp.maximum(i - 1, 0) * B_BLOCK if n_blocks > 1 else 0
            keys = pl.ds(pl.multiple_of(start, B_BLOCK), window)
            ahead = r_idx + (i * B_BLOCK - start) - c_idx
            valid = jnp.logical_and(ahead >= 0, ahead <= B_BLOCK)
            for p in range(n_pairs):
                lanes = lambda part: slice(base + part * B_GROUP_WIDTH + p * pair_width,
                                           base + part * B_GROUP_WIDTH + (p + 1) * pair_width)
                q2 = qkv_ref[cur, lanes(0)] * SCALE
                zero = jnp.zeros_like(q2)
                qs += [jnp.where(first_head, q2, zero), jnp.where(first_head, zero, q2)]
                ks += [qkv_ref[keys, lanes(1)]] * 2
                vs += [qkv_ref[keys, lanes(2)]] * 2
                valids += [valid] * 2
                dests.append((cur, slice(r * B_GROUP_WIDTH + p * pair_width,
                                         r * B_GROUP_WIDTH + (p + 1) * pair_width)))
        scores = [lax.dot_general(q, k, contract_last, preferred_element_type=F32) for q, k in zip(qs, ks)]
        probs, stats = [], []
        for s, valid in zip(scores, valids):
            s = jnp.where(valid, s, -jnp.inf)
            m = jnp.max(s, axis=1, keepdims=True)
            e = jnp.exp(s - m)
            l = jnp.sum(e, axis=1, keepdims=True)
            probs.append(e.astype(BF16))
            stats.append((m, l))
        outs = [jnp.dot(p, v, preferred_element_type=F32) for p, v in zip(probs, vs)]
        for d, (rows, lanes) in enumerate(dests):
            (m0, l0), (m1, l1) = stats[2 * d], stats[2 * d + 1]
            l = jnp.where(first_head, l0, l1)
            o_ref[rows, lanes] = jnp.where(first_head, outs[2 * d], outs[2 * d + 1]) / l
            lse_ref[rows, lanes] = jnp.where(first_head, m0, m1) + jnp.log(l)

    if n_blocks == 1:
        for r in range(0, dil, B_TASKS):
            blocks([(r + t, 0) for t in range(B_TASKS)])
    else:
        for r in range(dil):
            def body(step, carry, r=r):
                blocks([(r, step * B_TASKS + t) for t in range(B_TASKS)])
                return carry
            lax.fori_loop(0, n_blocks // B_TASKS, body, 0)


def _band_attention(qkvb, group, bsz):
    dil = B_PATTERNS[group][1]
    length = qkvb.shape[0] // bsz
    width = qkvb.shape[1] // dil
    view = qkvb.reshape(bsz, length, dil * width)
    out_spec = pl.BlockSpec((None, length, dil * B_GROUP_WIDTH), lambda b: (b, 0, 0))
    out_shape = jax.ShapeDtypeStruct((bsz, length, dil * B_GROUP_WIDTH), F32)
    o, lse = pl.pallas_call(
        functools.partial(_band_attention_kernel, dil=dil),
        grid=(bsz,),
        in_specs=[pl.BlockSpec((None, length, dil * width), lambda b: (b, 0, 0))],
        out_specs=[out_spec, out_spec],
        out_shape=[out_shape, out_shape],
        compiler_params=_params("parallel"),
        name=f"band_attention_g{group}",
    )(view)
    return o.reshape(bsz * length, dil * B_GROUP_WIDTH), lse.reshape(bsz * length, dil * B_GROUP_WIDTH)


def _stick_breaking_tiles(qs, ks, vs, upper, mask):
    contract_last = (((1,), (1,)), ((), ()))
    zs = [lax.dot_general(q, k, contract_last, preferred_element_type=F32) for q, k in zip(qs, ks)]
    log_keeps, logits, afters = [], [], []
    for nw in zs:
        log_keep = jnp.minimum(nw, 0.0) - jnp.log2(1.0 + jnp.exp2(-jnp.abs(nw)))
        if mask is not None:
            log_keep = jnp.where(mask, log_keep, 0.0)
        afters.append(jnp.dot(log_keep.astype(BF16), upper, preferred_element_type=F32))
        log_keeps.append(log_keep)
        logits.append(log_keep - nw)
    out = []
    for log_keep, logit, after, v in zip(log_keeps, logits, afters, vs):
        a = jnp.exp2(logit + after)
        if mask is not None:
            a = jnp.where(mask, a, 0.0)
        contrib = jnp.dot(a.astype(BF16), v, preferred_element_type=F32)
        out.append((contrib, jnp.sum(log_keep, axis=1, keepdims=True)))
    return out


def _stick_breaking_kernel(q_ref, k_ref, v_ref, o_ref, acc_ref, passed_ref):
    n_blocks = q_ref.shape[0] // C_BLOCK
    row = lax.broadcasted_iota(jnp.int32, (C_BLOCK, C_BLOCK), 0)
    col = lax.broadcasted_iota(jnp.int32, (C_BLOCK, C_BLOCK), 1)
    strictly_lower = col < row
    upper = jnp.where(row > col, 1.0, 0.0).astype(BF16)
    pair_width = 2 * HEAD_DIM
    pairs = [slice(p * pair_width, (p + 1) * pair_width) for p in range(C_HEADS_PER_STEP // 2)]
    first_head = lax.broadcasted_iota(jnp.int32, (C_BLOCK, pair_width), 1) < HEAD_DIM

    def q_block(i, carry):
        rows_i = pl.ds(pl.multiple_of(i * C_BLOCK, C_BLOCK), C_BLOCK)

        def tiles(rows_j, mask):
            qs, ks, vs = [], [], []
            for lanes in pairs:
                q2 = q_ref[rows_i, lanes]
                zero = jnp.zeros_like(q2)
                qs += [jnp.where(first_head, q2, zero), jnp.where(first_head, zero, q2)]
                ks += [k_ref[rows_j, lanes]] * 2
                vs += [v_ref[rows_j, lanes]] * 2
            res = _stick_breaking_tiles(qs, ks, vs, upper, mask)
            return [(jnp.where(first_head, res[2 * p][0], res[2 * p + 1][0]),
                     jnp.where(first_head, res[2 * p][1], res[2 * p + 1][1])) for p in range(len(pairs))]

        for p, (contrib, row_sum) in enumerate(tiles(rows_i, strictly_lower)):
            acc_ref[p] = contrib
            passed_ref[p] = row_sum

        def k_block(state):
            step, _ = state
            rows_j = pl.ds(pl.multiple_of((i - 1 - step) * C_BLOCK, C_BLOCK), C_BLOCK)
            most_alive = None
            for p, (contrib, row_sum) in enumerate(tiles(rows_j, None)):
                passed = passed_ref[p]
                acc_ref[p] += contrib * jnp.exp2(passed)
                passed = passed + row_sum
                passed_ref[p] = passed
                most_alive = passed if most_alive is None else jnp.maximum(most_alive, passed)
            return step + 1, jnp.max(most_alive)

        def more_to_do(state):
            step, most_alive = state
            return jnp.logical_and(step < i, most_alive > C_DEAD_LOG2)

        lax.while_loop(more_to_do, k_block, (jnp.int32(0), jnp.float32(0.0)))
        for p, lanes in enumerate(pairs):
            o_ref[rows_i, lanes] = acc_ref[p].astype(o_ref.dtype)
        return carry

    lax.fori_loop(0, n_blocks, q_block, 0)


def _stick_breaking(qkvc):
    bsz, s, _ = qkvc.shape
    width = C_HEADS_PER_STEP * HEAD_DIM
    n_steps = C_WIDTH // width

    def spec(part):
        return pl.BlockSpec((None, s, width), lambda b, p: (b, 0, part * n_steps + p))

    return pl.pallas_call(
        _stick_breaking_kernel,
        grid=(bsz, n_steps),
        in_specs=[spec(0), spec(1), spec(2)],
        out_specs=pl.BlockSpec((None, s, width), lambda b, p: (b, 0, p)),
        out_shape=jax.ShapeDtypeStruct((bsz, s, C_WIDTH), BF16),
        scratch_shapes=[pltpu.VMEM((C_HEADS_PER_STEP // 2, C_BLOCK, 2 * HEAD_DIM), F32),
                        pltpu.VMEM((C_HEADS_PER_STEP // 2, C_BLOCK, 2 * HEAD_DIM), F32)],
        compiler_params=_params("parallel", "parallel"),
        name="stick_breaking",
    )(qkvc, qkvc, qkvc)


def _fill_token_order(src_ref, scratch_ref, dil):
    tiles, rows, _ = scratch_ref.shape
    for r in range(dil):
        for c in range(tiles):
            lanes = slice((r * tiles + c) * LANES, (r * tiles + c + 1) * LANES)
            scratch_ref[c, pl.ds(r, rows // dil, stride=dil), :] = src_ref[:, lanes]


def _merge_kernel(x_ref, o0_ref, l0_ref, o1_ref, l1_ref, o2_ref, l2_ref, yc_ref,
                  pre_g_ref, w_in_ref, ln_g_ref, ln_b_ref, ws_ref, bs_ref,
                  wpa_ref, wpb_ref, wpc_ref, wo_ref, post_g_ref, out_ref, *order_refs):
    sub = MERGE_ROWS // MERGE_SUBTILES
    ranges = [slice(i * sub, (i + 1) * sub) for i in range(MERGE_SUBTILES)]
    dot = functools.partial(jnp.dot, preferred_element_type=F32)

    b_srcs = (o0_ref, l0_ref, o1_ref, l1_ref, o2_ref, l2_ref)
    b_dils = [dil for _, dil in B_PATTERNS for _ in range(2)]
    for src, scratch, dil in zip(b_srcs, order_refs, b_dils):
        if dil > 1:
            _fill_token_order(src, scratch, dil)

    def b_part(k, rows):
        if b_dils[k] == 1:
            return b_srcs[k][rows, :]
        return jnp.concatenate([order_refs[k][c, rows, :] for c in range(order_refs[k].shape[0])], axis=1)

    hs = [_rms(x_ref[rows, :], pre_g_ref[...]).astype(BF16) for rows in ranges]
    gate = lambda h, b: dot(h, w_in_ref[:, SPLIT_C + b * D_MODEL:SPLIT_C + (b + 1) * D_MODEL])
    zas = [dot(h, w_in_ref[:, :SPLIT_A]) for h in hs]
    gates_a = [gate(h, 0) for h in hs]

    us, vs = [], []
    for za in zas:
        z = jax.nn.gelu(za, approximate=True)
        v = z[:, A_WIDTH:]
        vc = v - jnp.mean(v, axis=-1, keepdims=True)
        v = vc * lax.rsqrt(jnp.mean(vc * vc, axis=-1, keepdims=True) + EPS) * ln_g_ref[...] + ln_b_ref[...]
        us.append(z[:, :A_WIDTH])
        vs.append(v.astype(BF16))
    gates_b = [gate(h, 1) for h in hs]
    t_idx = lax.broadcasted_iota(jnp.int32, (A_CHUNK, A_CHUNK), 0)
    s_idx = lax.broadcasted_iota(jnp.int32, (A_CHUNK, A_CHUNK), 1)
    ws = [jnp.where(s_idx <= t_idx, ws_ref[g], 0.0).astype(BF16) for g in range(A_GROUPS)]
    n_chunks = sub // A_CHUNK
    y_as = []
    for u, v in zip(us, vs):
        mixed = []
        for g in range(A_GROUPS):
            vg = jnp.concatenate([v[c * A_CHUNK:(c + 1) * A_CHUNK, g * A_GROUP_DIM:(g + 1) * A_GROUP_DIM]
                                  for c in range(n_chunks)], axis=1)
            mixed.append(dot(ws[g], vg))
        sv = jnp.concatenate(
            [jnp.concatenate([m[:, c * A_GROUP_DIM:(c + 1) * A_GROUP_DIM] for m in mixed], axis=1) + bs_ref[...]
             for c in range(n_chunks)], axis=0)
        y_as.append((u * sv).astype(BF16))

    y_bs = []
    for rows in ranges:
        o0, l0, o1, l1, o2, l2 = [b_part(k, rows) for k in range(6)]
        m = jnp.maximum(jnp.maximum(l0, l1), l2)
        w0, w1, w2 = jnp.exp(l0 - m), jnp.exp(l1 - m), jnp.exp(l2 - m)
        y_bs.append(((w0 * o0 + w1 * o1 + w2 * o2) / (w0 + w1 + w2)).astype(BF16))

    p_cs = [dot(yc_ref[rows, :], wpc_ref[...]) for rows in ranges]
    p_bs = [dot(y_b, wpb_ref[...]) for y_b in y_bs]
    gates_c = [gate(h, 2) for h in hs]
    p_as = [dot(y_a, wpa_ref[...]) for y_a in y_as]
    mergeds = [(_sigmoid(g_a) * p_a + _sigmoid(g_b) * p_b + _sigmoid(g_c) * p_c).astype(BF16)
               for g_a, p_a, g_b, p_b, g_c, p_c in zip(gates_a, p_as, gates_b, p_bs, gates_c, p_cs)]
    ys = [dot(merged, wo_ref[...]) for merged in mergeds]
    for rows, y in zip(ranges, ys):
        out_ref[rows, :] = x_ref[rows, :] + _rms(y, post_g_ref[...])


def _merge(x, b_parts, yc, layer, *params):
    n = x.shape[0]
    rows = lambda w: pl.BlockSpec((MERGE_ROWS, w), lambda i: (i, 0))
    b_flat = [a for part in b_parts for a in part]
    b_specs = [pl.BlockSpec((MERGE_ROWS // dil, dil * B_GROUP_WIDTH), lambda i: (i, 0))
               for _, dil in B_PATTERNS for _ in range(2)]
    return pl.pallas_call(
        _merge_kernel,
        grid=(n // MERGE_ROWS,),
        in_specs=[rows(D_MODEL)] + b_specs + [rows(C_WIDTH)] + [_resident(p, layer) for p in params],
        out_specs=rows(D_MODEL),
        out_shape=jax.ShapeDtypeStruct((n, D_MODEL), F32),
        scratch_shapes=[pltpu.VMEM((B_GROUP_WIDTH // LANES, MERGE_ROWS, LANES), F32)] * (2 * B_GROUPS),
        compiler_params=_params("parallel"),
        name="merge",
    )(x, *b_flat, yc, *params)


def kernel(x, ffn1_pre_g, ffn1_wi, ffn1_wo, ffn1_post_g, mix_pre_g, w_in, a_ln_g, a_ln_b, a_ws, a_bs,
           w_pa, w_pb, w_pc, w_o, mix_post_g, ffn2_pre_g, ffn2_wi, ffn2_wo, ffn2_post_g):
    bsz, s, d = x.shape
    n = bsz * s
    depth = ffn1_wi.shape[0]
    rows = lambda p: p[:, None, :]
    bf16 = lambda w: w.astype(BF16)
    ffn1 = (rows(ffn1_pre_g), bf16(ffn1_wi), bf16(ffn1_wo), rows(ffn1_post_g))
    ffn2 = (rows(ffn2_pre_g), bf16(ffn2_wi), bf16(ffn2_wo), rows(ffn2_post_g))
    w_in = bf16(w_in)
    mix_pre_g = rows(mix_pre_g)
    bs_full = jnp.repeat(jnp.swapaxes(a_bs, 1, 2), A_GROUP_DIM, axis=2)
    mix = (mix_pre_g, w_in, rows(a_ln_g), rows(a_ln_b), a_ws, bs_full,
           bf16(w_pa), bf16(w_pb), bf16(w_pc), bf16(w_o), rows(mix_post_g))

    x = x.reshape(n, d)
    for l in range(depth):
        x = _ffn(x, l, *ffn1)
        *qkvbs, qkvc = _proj(x, l, mix_pre_g, w_in)
        b_parts = [_band_attention(qkvb, g, bsz) for g, qkvb in enumerate(qkvbs)]
        yc = _stick_breaking(qkvc.reshape(bsz, s, 3 * C_WIDTH)).reshape(n, C_WIDTH)
        x = _merge(x, b_parts, yc, l, *mix)
        x = _ffn(x, l, *ffn2)
    return x.reshape(bsz, s, d)
```

```python
import functools

import jax
import jax.numpy as jnp
from jax import lax
from jax.experimental import pallas as pl
from jax.experimental.pallas import tpu as pltpu

D_MODEL = 1024
HEAD_DIM = 64
A_GROUPS = 4
A_CHUNK = 128
A_GROUP_DIM = 128
A_WIDTH = A_GROUPS * A_GROUP_DIM
B_PATTERNS = ((128, 1), (512, 4), (2048, 16))
B_GROUPS = len(B_PATTERNS)
B_HEADS_PER_GROUP = 4
B_GROUP_WIDTH = B_HEADS_PER_GROUP * HEAD_DIM
B_WIDTH = B_GROUPS * B_GROUP_WIDTH
B_BLOCK = 128
B_TASKS = 2
C_HEADS = 8
C_WIDTH = C_HEADS * HEAD_DIM
C_BLOCK = 256
C_HEADS_PER_STEP = 8
SPLIT_A = 2 * A_WIDTH
SPLIT_B = SPLIT_A + 3 * B_WIDTH
SPLIT_C = SPLIT_B + 3 * C_WIDTH
D_FF = 2816
EPS = 1e-6
SCALE = HEAD_DIM ** -0.5
LOG2E = 1.4426950408889634
C_Q_SCALE = -SCALE * LOG2E
C_DEAD_LOG2 = -160.0

F32 = jnp.float32
BF16 = jnp.bfloat16

LANES = 128

VMEM_LIMIT_BYTES = 56 * 1024 * 1024

FFN_ROWS = 1024
FFN_COLS = 256
FFN_SUBTILES = 4
PROJ_ROWS = 1024
MERGE_ROWS = 512
MERGE_SUBTILES = 2


def _params(*semantics):
    return pltpu.CompilerParams(dimension_semantics=semantics, vmem_limit_bytes=VMEM_LIMIT_BYTES)


def _rms(x, g):
    return x * lax.rsqrt(jnp.mean(x * x, axis=-1, keepdims=True) + EPS) * g


def _sigmoid(x):
    return 0.5 * jnp.tanh(0.5 * x) + 0.5


def _resident(stacked, layer):
    zeros = (0,) * (stacked.ndim - 1)
    return pl.BlockSpec((None, *stacked.shape[1:]), lambda *_: (layer, *zeros), pipeline_mode=pl.Buffered(1))


def _ffn_kernel(x_ref, pre_g_ref, wi_ref, wo_ref, post_g_ref, o_ref, act_ref):
    sub = FFN_ROWS // FFN_SUBTILES
    ranges = [slice(i * sub, (i + 1) * sub) for i in range(FFN_SUBTILES)]
    hs = [_rms(x_ref[rows, :], pre_g_ref[...]).astype(BF16) for rows in ranges]
    for rows, h in zip(ranges, hs):
        for c in range(D_FF // FFN_COLS):
            gate = jnp.dot(h, wi_ref[:, c * FFN_COLS:(c + 1) * FFN_COLS], preferred_element_type=F32)
            up = jnp.dot(h, wi_ref[:, D_FF + c * FFN_COLS:D_FF + (c + 1) * FFN_COLS], preferred_element_type=F32)
            act_ref[rows, c * FFN_COLS:(c + 1) * FFN_COLS] = (gate * jax.nn.sigmoid(gate) * up).astype(BF16)
    ys = [jnp.dot(act_ref[rows, :], wo_ref[...], preferred_element_type=F32) for rows in ranges]
    for rows, y in zip(ranges, ys):
        o_ref[rows, :] = x_ref[rows, :] + 0.5 * _rms(y, post_g_ref[...])


def _ffn(x, layer, pre_g, wi, wo, post_g):
    n = x.shape[0]
    rows = pl.BlockSpec((FFN_ROWS, D_MODEL), lambda i: (i, 0))
    return pl.pallas_call(
        _ffn_kernel,
        grid=(n // FFN_ROWS,),
        in_specs=[rows] + [_resident(p, layer) for p in (pre_g, wi, wo, post_g)],
        out_specs=rows,
        out_shape=jax.ShapeDtypeStruct((n, D_MODEL), F32),
        scratch_shapes=[pltpu.VMEM((FFN_ROWS, D_FF), BF16)],
        compiler_params=_params("parallel"),
        name="ffn",
    )(x, pre_g, wi, wo, post_g)


def _proj_kernel(x_ref, g_ref, w_in_ref, qkvb0_ref, qkvb1_ref, qkvb2_ref, qkvc_ref, class_ref):
    h = _rms(x_ref[...], g_ref[...]).astype(BF16)
    tiles = B_GROUP_WIDTH // LANES
    for g, (out_ref, (_, dil)) in enumerate(zip((qkvb0_ref, qkvb1_ref, qkvb2_ref), B_PATTERNS)):
        for part in range(3):
            first = SPLIT_A + part * B_WIDTH + g * B_GROUP_WIDTH
            y = jnp.dot(h, w_in_ref[:, first:first + B_GROUP_WIDTH], preferred_element_type=F32)
            if dil == 1:
                out_ref[:, part * B_GROUP_WIDTH:(part + 1) * B_GROUP_WIDTH] = y.astype(BF16)
            else:
                for c in range(tiles):
                    class_ref[part * tiles + c] = y[:, c * LANES:(c + 1) * LANES]
        if dil > 1:
            width = 3 * B_GROUP_WIDTH
            for r in range(dil):
                for c in range(width // LANES):
                    lanes = slice(r * width + c * LANES, r * width + (c + 1) * LANES)
                    out_ref[:, lanes] = class_ref[c, pl.ds(r, PROJ_ROWS // dil, stride=dil), :].astype(BF16)
    qkvc = jnp.dot(h, w_in_ref[:, SPLIT_B:SPLIT_C], preferred_element_type=F32)
    qkvc_ref[:, :C_WIDTH] = (qkvc[:, :C_WIDTH] * C_Q_SCALE).astype(BF16)
    qkvc_ref[:, C_WIDTH:] = qkvc[:, C_WIDTH:].astype(BF16)


def _proj(x, layer, g, w_in):
    n = x.shape[0]
    rows = lambda w, dil=1: pl.BlockSpec((PROJ_ROWS // dil, dil * w), lambda i: (i, 0))
    widths = [3 * B_GROUP_WIDTH] * B_GROUPS + [3 * C_WIDTH]
    dils = [*(dil for _, dil in B_PATTERNS), 1]
    return pl.pallas_call(
        _proj_kernel,
        grid=(n // PROJ_ROWS,),
        in_specs=[rows(D_MODEL), _resident(g, layer), _resident(w_in, layer)],
        out_specs=[rows(w, dil) for w, dil in zip(widths, dils)],
        out_shape=[jax.ShapeDtypeStruct((n // dil, dil * w), BF16) for w, dil in zip(widths, dils)],
        scratch_shapes=[pltpu.VMEM((3 * B_GROUP_WIDTH // LANES, PROJ_ROWS, LANES), F32)],
        compiler_params=_params("parallel"),
        name="proj",
    )(x, g, w_in)


def _band_attention_kernel(qkv_ref, o_ref, lse_ref, *, dil):
    length = qkv_ref.shape[0]
    n_blocks = length // B_BLOCK
    window = min(2 * B_BLOCK, length)
    r_idx = lax.broadcasted_iota(jnp.int32, (B_BLOCK, window), 0)
    c_idx = lax.broadcasted_iota(jnp.int32, (B_BLOCK, window), 1)
    first_head = lax.broadcasted_iota(jnp.int32, (B_BLOCK, 2 * HEAD_DIM), 1) < HEAD_DIM
    contract_last = (((1,), (1,)), ((), ()))
    pair_width = 2 * HEAD_DIM
    n_pairs = B_GROUP_WIDTH // pair_width

    def blocks(tasks):
        qs, ks, vs, valids, dests = [], [], [], [], []
        for r, i in tasks:
            base = r * 3 * B_GROUP_WIDTH
            cur = pl.ds(pl.multiple_of(i * B_BLOCK, B_BLOCK), B_BLOCK)
            start = jnp.maximum(i - 1, 0) * B_BLOCK if n_blocks > 1 else 0
            keys = pl.ds(pl.multiple_of(start, B_BLOCK), window)
            ahead = r_idx + (i * B_BLOCK - start) - c_idx
            valid = jnp.logical_and(ahead >= 0, ahead <= B_BLOCK)
            for p in range(n_pairs):
                lanes = lambda part: slice(base + part * B_GROUP_WIDTH + p * pair_width,
                                           base + part * B_GROUP_WIDTH + (p + 1) * pair_width)
                q2 = qkv_ref[cur, lanes(0)] * SCALE
                zero = jnp.zeros_like(q2)
                qs += [jnp.where(first_head, q2, zero), jnp.where(first_head, zero, q2)]
                ks += [qkv_ref[keys, lanes(1)]] * 2
                vs += [qkv_ref[keys, lanes(2)]] * 2
                valids += [valid] * 2
                dests.append((cur, slice(r * B_GROUP_WIDTH + p * pair_width,
                                         r * B_GROUP_WIDTH + (p + 1) * pair_width)))
        scores = [lax.dot_general(q, k, contract_last, preferred_element_type=F32) for q, k in zip(qs, ks)]
        probs, stats = [], []
        for s, valid in zip(scores, valids):
            s = jnp.where(valid, s, -jnp.inf)
            m = jnp.max(s, axis=1, keepdims=True)
            e = jnp.exp(s - m)
            l = jnp.sum(e, axis=1, keepdims=True)
            probs.append(e.astype(BF16))
            stats.append((m, l))
        outs = [jnp.dot(p, v, preferred_element_type=F32) for p, v in zip(probs, vs)]
        for d, (rows, lanes) in enumerate(dests):
            (m0, l0), (m1, l1) = stats[2 * d], stats[2 * d + 1]
            l = jnp.where(first_head, l0, l1)
            o_ref[rows, lanes] = jnp.where(first_head, outs[2 * d], outs[2 * d + 1]) / l
            lse_ref[rows, lanes] = jnp.where(first_head, m0, m1) + jnp.log(l)

    if n_blocks == 1:
        for r in range(0, dil, B_TASKS):
            blocks([(r + t, 0) for t in range(B_TASKS)])
    else:
        for r in range(dil):
            def body(step, carry, r=r):
                blocks([(r, step * B_TASKS + t) for t in range(B_TASKS)])
                return carry
            lax.fori_loop(0, n_blocks // B_TASKS, body, 0)


def _band_attention(qkvb, group, bsz):
    dil = B_PATTERNS[group][1]
    length = qkvb.shape[0] // bsz
    width = qkvb.shape[1] // dil
    view = qkvb.reshape(bsz, length, dil * width)
    out_spec = pl.BlockSpec((None, length, dil * B_GROUP_WIDTH), lambda b: (b, 0, 0))
    out_shape = jax.ShapeDtypeStruct((bsz, length, dil * B_GROUP_WIDTH), F32)
    o, lse = pl.pallas_call(
        functools.partial(_band_attention_kernel, dil=dil),
        grid=(bsz,),
        in_specs=[pl.BlockSpec((None, length, dil * width), lambda b: (b, 0, 0))],
        out_specs=[out_spec, out_spec],
        out_shape=[out_shape, out_shape],
        compiler_params=_params("parallel"),
        name=f"band_attention_g{group}",
    )(view)
    return o.reshape(bsz * length, dil * B_GROUP_WIDTH), lse.reshape(bsz * length, dil * B_GROUP_WIDTH)


def _stick_breaking_tiles(qs, ks, vs, upper, mask):
    contract_last = (((1,), (1,)), ((), ()))
    zs = [lax.dot_general(q, k, contract_last, preferred_element_type=F32) for q, k in zip(qs, ks)]
    log_keeps, logits, afters = [], [], []
    for nw in zs:
        log_keep = jnp.minimum(nw, 0.0) - jnp.log2(1.0 + jnp.exp2(-jnp.abs(nw)))
        if mask is not None:
            log_keep = jnp.where(mask, log_keep, 0.0)
        afters.append(jnp.dot(log_keep.astype(BF16), upper, preferred_element_type=F32))
        log_keeps.append(log_keep)
        logits.append(log_keep - nw)
    out = []
    for log_keep, logit, after, v in zip(log_keeps, logits, afters, vs):
        a = jnp.exp2(logit + after)
        if mask is not None:
            a = jnp.where(mask, a, 0.0)
        contrib = jnp.dot(a.astype(BF16), v, preferred_element_type=F32)
        out.append((contrib, jnp.sum(log_keep, axis=1, keepdims=True)))
    return out


def _stick_breaking_kernel(q_ref, k_ref, v_ref, o_ref, acc_ref, passed_ref):
    n_blocks = q_ref.shape[0] // C_BLOCK
    row = lax.broadcasted_iota(jnp.int32, (C_BLOCK, C_BLOCK), 0)
    col = lax.broadcasted_iota(jnp.int32, (C_BLOCK, C_BLOCK), 1)
    strictly_lower = col < row
    upper = jnp.where(row > col, 1.0, 0.0).astype(BF16)
    pair_width = 2 * HEAD_DIM
    pairs = [slice(p * pair_width, (p + 1) * pair_width) for p in range(C_HEADS_PER_STEP // 2)]
    first_head = lax.broadcasted_iota(jnp.int32, (C_BLOCK, pair_width), 1) < HEAD_DIM

    def q_block(i, carry):
        rows_i = pl.ds(pl.multiple_of(i * C_BLOCK, C_BLOCK), C_BLOCK)

        def tiles(rows_j, mask):
            qs, ks, vs = [], [], []
            for lanes in pairs:
                q2 = q_ref[rows_i, lanes]
                zero = jnp.zeros_like(q2)
                qs += [jnp.where(first_head, q2, zero), jnp.where(first_head, zero, q2)]
                ks += [k_ref[rows_j, lanes]] * 2
                vs += [v_ref[rows_j, lanes]] * 2
            res = _stick_breaking_tiles(qs, ks, vs, upper, mask)
            return [(jnp.where(first_head, res[2 * p][0], res[2 * p + 1][0]),
                     jnp.where(first_head, res[2 * p][1], res[2 * p + 1][1])) for p in range(len(pairs))]

        for p, (contrib, row_sum) in enumerate(tiles(rows_i, strictly_lower)):
            acc_ref[p] = contrib
            passed_ref[p] = row_sum

        def k_block(state):
            step, _ = state
            rows_j = pl.ds(pl.multiple_of((i - 1 - step) * C_BLOCK, C_BLOCK), C_BLOCK)
            most_alive = None
            for p, (contrib, row_sum) in enumerate(tiles(rows_j, None)):
                passed = passed_ref[p]
                acc_ref[p] += contrib * jnp.exp2(passed)
                passed = passed + row_sum
                passed_ref[p] = passed
                most_alive = passed if most_alive is None else jnp.maximum(most_alive, passed)
            return step + 1, jnp.max(most_alive)

        def more_to_do(state):
            step, most_alive = state
            return jnp.logical_and(step < i, most_alive > C_DEAD_LOG2)

        lax.while_loop(more_to_do, k_block, (jnp.int32(0), jnp.float32(0.0)))
        for p, lanes in enumerate(pairs):
            o_ref[rows_i, lanes] = acc_ref[p].astype(o_ref.dtype)
        return carry

    lax.fori_loop(0, n_blocks, q_block, 0)


def _stick_breaking(qkvc):
    bsz, s, _ = qkvc.shape
    width = C_HEADS_PER_STEP * HEAD_DIM
    n_steps = C_WIDTH // width

    def spec(part):
        return pl.BlockSpec((None, s, width), lambda b, p: (b, 0, part * n_steps + p))

    return pl.pallas_call(
        _stick_breaking_kernel,
        grid=(bsz, n_steps),
        in_specs=[spec(0), spec(1), spec(2)],
        out_specs=pl.BlockSpec((None, s, width), lambda b, p: (b, 0, p)),
        out_shape=jax.ShapeDtypeStruct((bsz, s, C_WIDTH), BF16),
        scratch_shapes=[pltpu.VMEM((C_HEADS_PER_STEP // 2, C_BLOCK, 2 * HEAD_DIM), F32),
                        pltpu.VMEM((C_HEADS_PER_STEP // 2, C_BLOCK, 2 * HEAD_DIM), F32)],
        compiler_params=_params("parallel", "parallel"),
        name="stick_breaking",
    )(qkvc, qkvc, qkvc)


def _fill_token_order(src_ref, scratch_ref, dil):
    tiles, rows, _ = scratch_ref.shape
    for r in range(dil):
        for c in range(tiles):
            lanes = slice((r * tiles + c) * LANES, (r * tiles + c + 1) * LANES)
            scratch_ref[c, pl.ds(r, rows // dil, stride=dil), :] = src_ref[:, lanes]


def _merge_kernel(x_ref, o0_ref, l0_ref, o1_ref, l1_ref, o2_ref, l2_ref, yc_ref,
                  pre_g_ref, w_in_ref, ln_g_ref, ln_b_ref, ws_ref, bs_ref,
                  wpa_ref, wpb_ref, wpc_ref, wo_ref, post_g_ref, out_ref, *order_refs):
    sub = MERGE_ROWS // MERGE_SUBTILES
    ranges = [slice(i * sub, (i + 1) * sub) for i in range(MERGE_SUBTILES)]
    dot = functools.partial(jnp.dot, preferred_element_type=F32)

    b_srcs = (o0_ref, l0_ref, o1_ref, l1_ref, o2_ref, l2_ref)
    b_dils = [dil for _, dil in B_PATTERNS for _ in range(2)]
    for src, scratch, dil in zip(b_srcs, order_refs, b_dils):
        if dil > 1:
            _fill_token_order(src, scratch, dil)

    def b_part(k, rows):
        if b_dils[k] == 1:
            return b_srcs[k][rows, :]
        return jnp.concatenate([order_refs[k][c, rows, :] for c in range(order_refs[k].shape[0])], axis=1)

    hs = [_rms(x_ref[rows, :], pre_g_ref[...]).astype(BF16) for rows in ranges]
    gate = lambda h, b: dot(h, w_in_ref[:, SPLIT_C + b * D_MODEL:SPLIT_C + (b + 1) * D_MODEL])
    zas = [dot(h, w_in_ref[:, :SPLIT_A]) for h in hs]
    gates_a = [gate(h, 0) for h in hs]

    us, vs = [], []
    for za in zas:
        z = jax.nn.gelu(za, approximate=True)
        v = z[:, A_WIDTH:]
        vc = v - jnp.mean(v, axis=-1, keepdims=True)
        v = vc * lax.rsqrt(jnp.mean(vc * vc, axis=-1, keepdims=True) + EPS) * ln_g_ref[...] + ln_b_ref[...]
        us.append(z[:, :A_WIDTH])
        vs.append(v.astype(BF16))
    gates_b = [gate(h, 1) for h in hs]
    t_idx = lax.broadcasted_iota(jnp.int32, (A_CHUNK, A_CHUNK), 0)
    s_idx = lax.broadcasted_iota(jnp.int32, (A_CHUNK, A_CHUNK), 1)
    ws = [jnp.where(s_idx <= t_idx, ws_ref[g], 0.0).astype(BF16) for g in range(A_GROUPS)]
    n_chunks = sub // A_CHUNK
    y_as = []
    for u, v in zip(us, vs):
        mixed = []
        for g in range(A_GROUPS):
            vg = jnp.concatenate([v[c * A_CHUNK:(c + 1) * A_CHUNK, g * A_GROUP_DIM:(g + 1) * A_GROUP_DIM]
                                  for c in range(n_chunks)], axis=1)
            mixed.append(dot(ws[g], vg))
        sv = jnp.concatenate(
            [jnp.concatenate([m[:, c * A_GROUP_DIM:(c + 1) * A_GROUP_DIM] for m in mixed], axis=1) + bs_ref[...]
             for c in range(n_chunks)], axis=0)
        y_as.append((u * sv).astype(BF16))

    y_bs = []
    for rows in ranges:
        o0, l0, o1, l1, o2, l2 = [b_part(k, rows) for k in range(6)]
        m = jnp.maximum(jnp.maximum(l0, l1), l2)
        w0, w1, w2 = jnp.exp(l0 - m), jnp.exp(l1 - m), jnp.exp(l2 - m)
        y_bs.append(((w0 * o0 + w1 * o1 + w2 * o2) / (w0 + w1 + w2)).astype(BF16))

    p_cs = [dot(yc_ref[rows, :], wpc_ref[...]) for rows in ranges]
    p_bs = [dot(y_b, wpb_ref[...]) for y_b in y_bs]
    gates_c = [gate(h, 2) for h in hs]
    p_as = [dot(y_a, wpa_ref[...]) for y_a in y_as]
    mergeds = [(_sigmoid(g_a) * p_a + _sigmoid(g_b) * p_b + _sigmoid(g_c) * p_c).astype(BF16)
               for g_a, p_a, g_b, p_b, g_c, p_c in zip(gates_a, p_as, gates_b, p_bs, gates_c, p_cs)]
    ys = [dot(merged, wo_ref[...]) for merged in mergeds]
    for rows, y in zip(ranges, ys):
        out_ref[rows, :] = x_ref[rows, :] + _rms(y, post_g_ref[...])


def _merge(x, b_parts, yc, layer, *params):
    n = x.shape[0]
    rows = lambda w: pl.BlockSpec((MERGE_ROWS, w), lambda i: (i, 0))
    b_flat = [a for part in b_parts for a in part]
    b_specs = [pl.BlockSpec((MERGE_ROWS // dil, dil * B_GROUP_WIDTH), lambda i: (i, 0))
               for _, dil in B_PATTERNS for _ in range(2)]
    return pl.pallas_call(
        _merge_kernel,
        grid=(n // MERGE_ROWS,),
        in_specs=[rows(D_MODEL)] + b_specs + [rows(C_WIDTH)] + [_resident(p, layer) for p in params],
        out_specs=rows(D_MODEL),
        out_shape=jax.ShapeDtypeStruct((n, D_MODEL), F32),
        scratch_shapes=[pltpu.VMEM((B_GROUP_WIDTH // LANES, MERGE_ROWS, LANES), F32)] * (2 * B_GROUPS),
        compiler_params=_params("parallel"),
        name="merge",
    )(x, *b_flat, yc, *params)


def kernel(x, ffn1_pre_g, ffn1_wi, ffn1_wo, ffn1_post_g, mix_pre_g, w_in, a_ln_g, a_ln_b, a_ws, a_bs,
           w_pa, w_pb, w_pc, w_o, mix_post_g, ffn2_pre_g, ffn2_wi, ffn2_wo, ffn2_post_g):
    bsz, s, d = x.shape
    n = bsz * s
    depth = ffn1_wi.shape[0]
    rows = lambda p: p[:, None, :]
    bf16 = lambda w: w.astype(BF16)
    ffn1 = (rows(ffn1_pre_g), bf16(ffn1_wi), bf16(ffn1_wo), rows(ffn1_post_g))
    ffn2 = (rows(ffn2_pre_g), bf16(ffn2_wi), bf16(ffn2_wo), rows(ffn2_post_g))
    w_in = bf16(w_in)
    mix_pre_g = rows(mix_pre_g)
    bs_full = jnp.repeat(jnp.swapaxes(a_bs, 1, 2), A_GROUP_DIM, axis=2)
    mix = (mix_pre_g, w_in, rows(a_ln_g), rows(a_ln_b), a_ws, bs_full,
           bf16(w_pa), bf16(w_pb), bf16(w_pc), bf16(w_o), rows(mix_post_g))

    x = x.reshape(n, d)
    for l in range(depth):
        x = _ffn(x, l, *ffn1)
        *qkvbs, qkvc = _proj(x, l, mix_pre_g, w_in)
        b_parts = [_band_attention(qkvb, g, bsz) for g, qkvb in enumerate(qkvbs)]
        yc = _stick_breaking(qkvc.reshape(bsz, s, 3 * C_WIDTH)).reshape(n, C_WIDTH)
        x = _merge(x, b_parts, yc, l, *mix)
        x = _ffn(x, l, *ffn2)
    return x.reshape(bsz, s, d)
```

```python
import functools

import jax
import jax.numpy as jnp
from jax import lax
from jax.experimental import pallas as pl
from jax.experimental.pallas import tpu as pltpu

D_MODEL = 1024
HEAD_DIM = 64
A_GROUPS = 4
A_CHUNK = 128
A_GROUP_DIM = 128
A_WIDTH = A_GROUPS * A_GROUP_DIM
B_PATTERNS = ((128, 1), (512, 4), (2048, 16))
B_GROUPS = len(B_PATTERNS)
B_HEADS_PER_GROUP = 4
B_GROUP_WIDTH = B_HEADS_PER_GROUP * HEAD_DIM
B_WIDTH = B_GROUPS * B_GROUP_WIDTH
B_BLOCK = 128
B_TASKS = 2
C_HEADS = 8
C_WIDTH = C_HEADS * HEAD_DIM
C_BLOCK = 256
C_HEADS_PER_STEP = 8
SPLIT_A = 2 * A_WIDTH
SPLIT_B = SPLIT_A + 3 * B_WIDTH
SPLIT_C = SPLIT_B + 3 * C_WIDTH
D_FF = 2816
EPS = 1e-6
SCALE = HEAD_DIM ** -0.5
LOG2E = 1.4426950408889634
C_Q_SCALE = -SCALE * LOG2E
C_DEAD_LOG2 = -160.0

F32 = jnp.float32
BF16 = jnp.bfloat16

LANES = 128

VMEM_LIMIT_BYTES = 60 * 1024 * 1024

FFN_ROWS = 1024
FFN_COLS = 256
FFN_SUBTILES = 4
PROJ_ROWS = 1024
MERGE_ROWS = 512
MERGE_SUBTILES = 2


def _params(*semantics):
    return pltpu.CompilerParams(dimension_semantics=semantics, vmem_limit_bytes=VMEM_LIMIT_BYTES)


def _rms(x, g):
    return x * lax.rsqrt(jnp.mean(x * x, axis=-1, keepdims=True) + EPS) * g


def _sigmoid(x):
    return 0.5 * jnp.tanh(0.5 * x) + 0.5


def _resident(param):
    stacked, layer = param
    zeros = (0,) * (stacked.ndim - 1)
    return pl.BlockSpec((None, *stacked.shape[1:]), lambda *_: (layer, *zeros), pipeline_mode=pl.Buffered(1))


def _ffn_kernel(x_ref, pre_g_ref, wi_ref, wo_ref, post_g_ref, *refs):
    n_casts = (len(refs) - 2) // 2
    o_ref, act_ref = refs[n_casts], refs[-1]
    for src_ref, dst_ref in zip(refs[:n_casts], refs[n_casts + 1:-1]):
        dst_ref[...] = src_ref[...].astype(BF16)

    sub = FFN_ROWS // FFN_SUBTILES
    ranges = [slice(i * sub, (i + 1) * sub) for i in range(FFN_SUBTILES)]
    hs = [_rms(x_ref[rows, :], pre_g_ref[...]).astype(BF16) for rows in ranges]
    for rows, h in zip(ranges, hs):
        for c in range(D_FF // FFN_COLS):
            gate = jnp.dot(h, wi_ref[:, c * FFN_COLS:(c + 1) * FFN_COLS], preferred_element_type=F32)
            up = jnp.dot(h, wi_ref[:, D_FF + c * FFN_COLS:D_FF + (c + 1) * FFN_COLS], preferred_element_type=F32)
            act_ref[rows, c * FFN_COLS:(c + 1) * FFN_COLS] = (gate * jax.nn.sigmoid(gate) * up).astype(BF16)
    ys = [jnp.dot(act_ref[rows, :], wo_ref[...], preferred_element_type=F32) for rows in ranges]
    for rows, y in zip(ranges, ys):
        o_ref[rows, :] = x_ref[rows, :] + 0.5 * _rms(y, post_g_ref[...])


def _ffn(x, pre_g, wi, wo, post_g, casts=()):
    n = x.shape[0]
    steps = n // FFN_ROWS
    rows = pl.BlockSpec((FFN_ROWS, D_MODEL), lambda i: (i, 0))
    params = (pre_g, wi, wo, post_g)
    cast_in, cast_out, cast_shapes = [], [], []
    for stacked, layer in casts:
        r, c = stacked.shape[1:]
        block = (None, r // steps, c)
        cast_in.append(pl.BlockSpec(block, lambda i, layer=layer: (layer, i, 0)))
        cast_out.append(pl.BlockSpec(block, lambda i: (0, i, 0)))
        cast_shapes.append(jax.ShapeDtypeStruct((1, r, c), BF16))
    out, *converted = pl.pallas_call(
        _ffn_kernel,
        grid=(steps,),
        in_specs=[rows] + [_resident(p) for p in params] + cast_in,
        out_specs=[rows] + cast_out,
        out_shape=[jax.ShapeDtypeStruct((n, D_MODEL), F32)] + cast_shapes,
        scratch_shapes=[pltpu.VMEM((FFN_ROWS, D_FF), BF16)],
        compiler_params=_params("parallel"),
        name="ffn",
    )(x, *(p[0] for p in params), *(p[0] for p in casts))
    return out, [(w, 0) for w in converted]


def _proj_kernel(x_ref, g_ref, w_in_ref, qkvb0_ref, qkvb1_ref, qkvb2_ref, qkvc_ref, class_ref):
    h = _rms(x_ref[...], g_ref[...]).astype(BF16)
    tiles = B_GROUP_WIDTH // LANES
    for g, (out_ref, (_, dil)) in enumerate(zip((qkvb0_ref, qkvb1_ref, qkvb2_ref), B_PATTERNS)):
        for part in range(3):
            first = SPLIT_A + part * B_WIDTH + g * B_GROUP_WIDTH
            y = jnp.dot(h, w_in_ref[:, first:first + B_GROUP_WIDTH], preferred_element_type=F32)
            if dil == 1:
                out_ref[:, part * B_GROUP_WIDTH:(part + 1) * B_GROUP_WIDTH] = y.astype(BF16)
            else:
                for c in range(tiles):
                    class_ref[part * tiles + c] = y[:, c * LANES:(c + 1) * LANES]
        if dil > 1:
            width = 3 * B_GROUP_WIDTH
            for r in range(dil):
                for c in range(width // LANES):
                    lanes = slice(r * width + c * LANES, r * width + (c + 1) * LANES)
                    out_ref[:, lanes] = class_ref[c, pl.ds(r, PROJ_ROWS // dil, stride=dil), :].astype(BF16)
    qkvc = jnp.dot(h, w_in_ref[:, SPLIT_B:SPLIT_C], preferred_element_type=F32)
    qkvc_ref[:, :C_WIDTH] = (qkvc[:, :C_WIDTH] * C_Q_SCALE).astype(BF16)
    qkvc_ref[:, C_WIDTH:] = qkvc[:, C_WIDTH:].astype(BF16)


def _proj(x, g, w_in):
    n = x.shape[0]
    rows = lambda w, dil=1: pl.BlockSpec((PROJ_ROWS // dil, dil * w), lambda i: (i, 0))
    widths = [3 * B_GROUP_WIDTH] * B_GROUPS + [3 * C_WIDTH]
    dils = [*(dil for _, dil in B_PATTERNS), 1]
    return pl.pallas_call(
        _proj_kernel,
        grid=(n // PROJ_ROWS,),
        in_specs=[rows(D_MODEL), _resident(g), _resident(w_in)],
        out_specs=[rows(w, dil) for w, dil in zip(widths, dils)],
        out_shape=[jax.ShapeDtypeStruct((n // dil, dil * w), BF16) for w, dil in zip(widths, dils)],
        scratch_shapes=[pltpu.VMEM((3 * B_GROUP_WIDTH // LANES, PROJ_ROWS, LANES), F32)],
        compiler_params=_params("parallel"),
        name="proj",
    )(x, g[0], w_in[0])


def _band_attention_kernel(qkv_ref, o_ref, lse_ref, *, dil):
    length = qkv_ref.shape[0]
    n_blocks = length // B_BLOCK
    window = min(2 * B_BLOCK, length)
    r_idx = lax.broadcasted_iota(jnp.int32, (B_BLOCK, window), 0)
    c_idx = lax.broadcasted_iota(jnp.int32, (B_BLOCK, window), 1)
    first_head = lax.broadcasted_iota(jnp.int32, (B_BLOCK, 2 * HEAD_DIM), 1) < HEAD_DIM
    contract_last = (((1,), (1,)), ((), ()))
    pair_width = 2 * HEAD_DIM
    n_pairs = B_GROUP_WIDTH // pair_width

    def blocks(tasks):
        qs, ks, vs, valids, dests = [], [], [], [], []
        for r, i in tasks:
            base = r * 3 * B_GROUP_WIDTH
            cur = pl.ds(pl.multiple_of(i * B_BLOCK, B_BLOCK), B_BLOCK)
            start = jnp.maximum(i - 1, 0) * B_BLOCK if n_blocks > 1 else 0
            keys = pl.ds(pl.multiple_of(start, B_BLOCK), window)
            ahead = r_idx + (i * B_BLOCK - start) - c_idx
            valid = jnp.logical_and(ahead >= 0, ahead <= B_BLOCK)
            for p in range(n_pairs):
                lanes = lambda part: slice(base + part * B_GROUP_WIDTH + p * pair_width,
                                           base + part * B_GROUP_WIDTH + (p + 1) * pair_width)
                q2 = qkv_ref[cur, lanes(0)] * SCALE
                zero = jnp.zeros_like(q2)
                qs += [jnp.where(first_head, q2, zero), jnp.where(first_head, zero, q2)]
                ks += [qkv_ref[keys, lanes(1)]] * 2
                vs += [qkv_ref[keys, lanes(2)]] * 2
                valids += [valid] * 2
                dests.append((cur, slice(r * B_GROUP_WIDTH + p * pair_width,
                                         r * B_GROUP_WIDTH + (p + 1) * pair_width)))
        scores = [lax.dot_general(q, k, contract_last, preferred_element_type=F32) for q, k in zip(qs, ks)]
        probs, stats = [], []
        for s, valid in zip(scores, valids):
            s = jnp.where(valid, s, -jnp.inf)
            m = jnp.max(s, axis=1, keepdims=True)
            e = jnp.exp(s - m)
            l = jnp.sum(e, axis=1, keepdims=True)
            probs.append(e.astype(BF16))
            stats.append((m, l))
        outs = [jnp.dot(p, v, preferred_element_type=F32) for p, v in zip(probs, vs)]
        for d, (rows, lanes) in enumerate(dests):
            (m0, l0), (m1, l1) = stats[2 * d], stats[2 * d + 1]
            l = jnp.where(first_head, l0, l1)
            o_ref[rows, lanes] = jnp.where(first_head, outs[2 * d], outs[2 * d + 1]) / l
            lse_ref[rows, lanes] = jnp.where(first_head, m0, m1) + jnp.log(l)

    if n_blocks == 1:
        for r in range(0, dil, B_TASKS):
            blocks([(r + t, 0) for t in range(B_TASKS)])
    else:
        for r in range(dil):
            def body(step, carry, r=r):
                blocks([(r, step * B_TASKS + t) for t in range(B_TASKS)])
                return carry
            lax.fori_loop(0, n_blocks // B_TASKS, body, 0)


def _band_attention(qkvb, group, bsz):
    dil = B_PATTERNS[group][1]
    length = qkvb.shape[0] // bsz
    width = qkvb.shape[1] // dil
    view = qkvb.reshape(bsz, length, dil * width)
    out_spec = pl.BlockSpec((None, length, dil * B_GROUP_WIDTH), lambda b: (b, 0, 0))
    out_shape = jax.ShapeDtypeStruct((bsz, length, dil * B_GROUP_WIDTH), F32)
    o, lse = pl.pallas_call(
        functools.partial(_band_attention_kernel, dil=dil),
        grid=(bsz,),
        in_specs=[pl.BlockSpec((None, length, dil * width), lambda b: (b, 0, 0))],
        out_specs=[out_spec, out_spec],
        out_shape=[out_shape, out_shape],
        compiler_params=_params("parallel"),
        name=f"band_attention_g{group}",
    )(view)
    return o.reshape(bsz * length, dil * B_GROUP_WIDTH), lse.reshape(bsz * length, dil * B_GROUP_WIDTH)


def _stick_breaking_tiles(qs, ks, vs, upper, mask):
    contract_last = (((1,), (1,)), ((), ()))
    zs = [lax.dot_general(q, k, contract_last, preferred_element_type=F32) for q, k in zip(qs, ks)]
    log_keeps, logits, afters = [], [], []
    for nw in zs:
        log_keep = jnp.minimum(nw, 0.0) - jnp.log2(1.0 + jnp.exp2(-jnp.abs(nw)))
        if mask is not None:
            log_keep = jnp.where(mask, log_keep, 0.0)
        afters.append(jnp.dot(log_keep.astype(BF16), upper, preferred_element_type=F32))
        log_keeps.append(log_keep)
        logits.append(log_keep - nw)
    out = []
    for log_keep, logit, after, v in zip(log_keeps, logits, afters, vs):
        a = jnp.exp2(logit + after)
        if mask is not None:
            a = jnp.where(mask, a, 0.0)
        contrib = jnp.dot(a.astype(BF16), v, preferred_element_type=F32)
        out.append((contrib, jnp.sum(log_keep, axis=1, keepdims=True)))
    return out


def _stick_breaking_kernel(q_ref, k_ref, v_ref, o_ref, acc_ref, passed_ref):
    n_blocks = q_ref.shape[0] // C_BLOCK
    row = lax.broadcasted_iota(jnp.int32, (C_BLOCK, C_BLOCK), 0)
    col = lax.broadcasted_iota(jnp.int32, (C_BLOCK, C_BLOCK), 1)
    strictly_lower = col < row
    upper = jnp.where(row > col, 1.0, 0.0).astype(BF16)
    pair_width = 2 * HEAD_DIM
    pairs = [slice(p * pair_width, (p + 1) * pair_width) for p in range(C_HEADS_PER_STEP // 2)]
    first_head = lax.broadcasted_iota(jnp.int32, (C_BLOCK, pair_width), 1) < HEAD_DIM

    def q_block(i, carry):
        rows_i = pl.ds(pl.multiple_of(i * C_BLOCK, C_BLOCK), C_BLOCK)

        def tiles(rows_j, mask):
            qs, ks, vs = [], [], []
            for lanes in pairs:
                q2 = q_ref[rows_i, lanes]
                zero = jnp.zeros_like(q2)
                qs += [jnp.where(first_head, q2, zero), jnp.where(first_head, zero, q2)]
                ks += [k_ref[rows_j, lanes]] * 2
                vs += [v_ref[rows_j, lanes]] * 2
            res = _stick_breaking_tiles(qs, ks, vs, upper, mask)
            return [(jnp.where(first_head, res[2 * p][0], res[2 * p + 1][0]),
                     jnp.where(first_head, res[2 * p][1], res[2 * p + 1][1])) for p in range(len(pairs))]

        for p, (contrib, row_sum) in enumerate(tiles(rows_i, strictly_lower)):
            acc_ref[p] = contrib
            passed_ref[p] = row_sum

        def k_block(state):
            step, _ = state
            rows_j = pl.ds(pl.multiple_of((i - 1 - step) * C_BLOCK, C_BLOCK), C_BLOCK)
            most_alive = None
            for p, (contrib, row_sum) in enumerate(tiles(rows_j, None)):
                passed = passed_ref[p]
                acc_ref[p] += contrib * jnp.exp2(passed)
                passed = passed + row_sum
                passed_ref[p] = passed
                most_alive = passed if most_alive is None else jnp.maximum(most_alive, passed)
            return step + 1, jnp.max(most_alive)

        def more_to_do(state):
            step, most_alive = state
            return jnp.logical_and(step < i, most_alive > C_DEAD_LOG2)

        lax.while_loop(more_to_do, k_block, (jnp.int32(0), jnp.float32(0.0)))
        for p, lanes in enumerate(pairs):
            o_ref[rows_i, lanes] = acc_ref[p].astype(o_ref.dtype)
        return carry

    lax.fori_loop(0, n_blocks, q_block, 0)


def _stick_breaking(qkvc):
    bsz, s, _ = qkvc.shape
    width = C_HEADS_PER_STEP * HEAD_DIM
    n_steps = C_WIDTH // width

    def spec(part):
        return pl.BlockSpec((None, s, width), lambda b, p: (b, 0, part * n_steps + p))

    return pl.pallas_call(
        _stick_breaking_kernel,
        grid=(bsz, n_steps),
        in_specs=[spec(0), spec(1), spec(2)],
        out_specs=pl.BlockSpec((None, s, width), lambda b, p: (b, 0, p)),
        out_shape=jax.ShapeDtypeStruct((bsz, s, C_WIDTH), BF16),
        scratch_shapes=[pltpu.VMEM((C_HEADS_PER_STEP // 2, C_BLOCK, 2 * HEAD_DIM), F32),
                        pltpu.VMEM((C_HEADS_PER_STEP // 2, C_BLOCK, 2 * HEAD_DIM), F32)],
        compiler_params=_params("parallel", "parallel"),
        name="stick_breaking",
    )(qkvc, qkvc, qkvc)


def _fill_token_order(src_ref, scratch_ref, dil):
    tiles, rows, _ = scratch_ref.shape
    for r in range(dil):
        for c in range(tiles):
            lanes = slice((r * tiles + c) * LANES, (r * tiles + c + 1) * LANES)
            scratch_ref[c, pl.ds(r, rows // dil, stride=dil), :] = src_ref[:, lanes]


def _merge_kernel(x_ref, o0_ref, l0_ref, o1_ref, l1_ref, o2_ref, l2_ref, yc_ref,
                  pre_g_ref, w_in_ref, ln_g_ref, ln_b_ref, ws_ref, bs_ref,
                  wpa_ref, wpb_ref, wpc_ref, wo_ref, post_g_ref, out_ref, *order_refs):
    sub = MERGE_ROWS // MERGE_SUBTILES
    ranges = [slice(i * sub, (i + 1) * sub) for i in range(MERGE_SUBTILES)]
    dot = functools.partial(jnp.dot, preferred_element_type=F32)

    b_srcs = (o0_ref, l0_ref, o1_ref, l1_ref, o2_ref, l2_ref)
    b_dils = [dil for _, dil in B_PATTERNS for _ in range(2)]
    for src, scratch, dil in zip(b_srcs, order_refs, b_dils):
        if dil > 1:
            _fill_token_order(src, scratch, dil)

    def b_part(k, rows):
        if b_dils[k] == 1:
            return b_srcs[k][rows, :]
        return jnp.concatenate([order_refs[k][c, rows, :] for c in range(order_refs[k].shape[0])], axis=1)

    hs = [_rms(x_ref[rows, :], pre_g_ref[...]).astype(BF16) for rows in ranges]
    gate = lambda h, b: dot(h, w_in_ref[:, SPLIT_C + b * D_MODEL:SPLIT_C + (b + 1) * D_MODEL])
    zas = [dot(h, w_in_ref[:, :SPLIT_A]) for h in hs]
    gates_a = [gate(h, 0) for h in hs]

    us, vs = [], []
    for za in zas:
        z = jax.nn.gelu(za, approximate=True)
        v = z[:, A_WIDTH:]
        vc = v - jnp.mean(v, axis=-1, keepdims=True)
        v = vc * lax.rsqrt(jnp.mean(vc * vc, axis=-1, keepdims=True) + EPS) * ln_g_ref[...] + ln_b_ref[...]
        us.append(z[:, :A_WIDTH])
        vs.append(v.astype(BF16))
    gates_b = [gate(h, 1) for h in hs]
    t_idx = lax.broadcasted_iota(jnp.int32, (A_CHUNK, A_CHUNK), 0)
    s_idx = lax.broadcasted_iota(jnp.int32, (A_CHUNK, A_CHUNK), 1)
    ws = [jnp.where(s_idx <= t_idx, ws_ref[g], 0.0).astype(BF16) for g in range(A_GROUPS)]
    n_chunks = sub // A_CHUNK
    y_as = []
    for u, v in zip(us, vs):
        mixed = []
        for g in range(A_GROUPS):
            vg = jnp.concatenate([v[c * A_CHUNK:(c + 1) * A_CHUNK, g * A_GROUP_DIM:(g + 1) * A_GROUP_DIM]
                                  for c in range(n_chunks)], axis=1)
            mixed.append(dot(ws[g], vg))
        sv = jnp.concatenate(
            [jnp.concatenate([m[:, c * A_GROUP_DIM:(c + 1) * A_GROUP_DIM] for m in mixed], axis=1) + bs_ref[...]
             for c in range(n_chunks)], axis=0)
        y_as.append((u * sv).astype(BF16))

    y_bs = []
    for rows in ranges:
        o0, l0, o1, l1, o2, l2 = [b_part(k, rows) for k in range(6)]
        m = jnp.maximum(jnp.maximum(l0, l1), l2)
        w0, w1, w2 = jnp.exp(l0 - m), jnp.exp(l1 - m), jnp.exp(l2 - m)
        y_bs.append(((w0 * o0 + w1 * o1 + w2 * o2) / (w0 + w1 + w2)).astype(BF16))

    p_cs = [dot(yc_ref[rows, :], wpc_ref[...]) for rows in ranges]
    p_bs = [dot(y_b, wpb_ref[...]) for y_b in y_bs]
    gates_c = [gate(h, 2) for h in hs]
    p_as = [dot(y_a, wpa_ref[...]) for y_a in y_as]
    mergeds = [(_sigmoid(g_a) * p_a + _sigmoid(g_b) * p_b + _sigmoid(g_c) * p_c).astype(BF16)
               for g_a, p_a, g_b, p_b, g_c, p_c in zip(gates_a, p_as, gates_b, p_bs, gates_c, p_cs)]
    ys = [dot(merged, wo_ref[...]) for merged in mergeds]
    for rows, y in zip(ranges, ys):
        out_ref[rows, :] = x_ref[rows, :] + _rms(y, post_g_ref[...])


def _merge(x, b_parts, yc, *params):
    n = x.shape[0]
    rows = lambda w: pl.BlockSpec((MERGE_ROWS, w), lambda i: (i, 0))
    b_flat = [a for part in b_parts for a in part]
    b_specs = [pl.BlockSpec((MERGE_ROWS // dil, dil * B_GROUP_WIDTH), lambda i: (i, 0))
               for _, dil in B_PATTERNS for _ in range(2)]
    return pl.pallas_call(
        _merge_kernel,
        grid=(n // MERGE_ROWS,),
        in_specs=[rows(D_MODEL)] + b_specs + [rows(C_WIDTH)] + [_resident(p) for p in params],
        out_specs=rows(D_MODEL),
        out_shape=jax.ShapeDtypeStruct((n, D_MODEL), F32),
        scratch_shapes=[pltpu.VMEM((B_GROUP_WIDTH // LANES, MERGE_ROWS, LANES), F32)] * (2 * B_GROUPS),
        compiler_params=_params("parallel"),
        name="merge",
    )(x, *b_flat, yc, *(p[0] for p in params))


def kernel(x, ffn1_pre_g, ffn1_wi, ffn1_wo, ffn1_post_g, mix_pre_g, w_in, a_ln_g, a_ln_b, a_ws, a_bs,
           w_pa, w_pb, w_pc, w_o, mix_post_g, ffn2_pre_g, ffn2_wi, ffn2_wo, ffn2_post_g):
    bsz, s, d = x.shape
    n = bsz * s
    depth = ffn1_wi.shape[0]
    rows = lambda p: p[:, None, :]
    ffn1_g, ffn2_g = (rows(ffn1_pre_g), rows(ffn1_post_g)), (rows(ffn2_pre_g), rows(ffn2_post_g))
    mix_pre_g, mix_post_g, a_ln_g, a_ln_b = rows(mix_pre_g), rows(mix_post_g), rows(a_ln_g), rows(a_ln_b)
    bs_full = jnp.repeat(jnp.swapaxes(a_bs, 1, 2), A_GROUP_DIM, axis=2)
    mixer_weights = (w_in, w_pa, w_pb, w_pc, w_o)
    ffn1_w = [(ffn1_wi[:1].astype(BF16), 0), (ffn1_wo[:1].astype(BF16), 0)]

    x = x.reshape(n, d)
    for l in range(depth):
        x, converted = _ffn(x, (ffn1_g[0], l), *ffn1_w, (ffn1_g[1], l),
                            casts=[(w, l) for w in (*mixer_weights, ffn2_wi, ffn2_wo)])
        (w_in_l, w_pa_l, w_pb_l, w_pc_l, w_o_l), ffn2_w = converted[:5], converted[5:]
        *qkvbs, qkvc = _proj(x, (mix_pre_g, l), w_in_l)
        b_parts = [_band_attention(qkvb, g, bsz) for g, qkvb in enumerate(qkvbs)]
        yc = _stick_breaking(qkvc.reshape(bsz, s, 3 * C_WIDTH)).reshape(n, C_WIDTH)
        x = _merge(x, b_parts, yc, (mix_pre_g, l), w_in_l, (a_ln_g, l), (a_ln_b, l), (a_ws, l), (bs_full, l),
                   w_pa_l, w_pb_l, w_pc_l, w_o_l, (mix_post_g, l))
        next_ffn1 = [(w, l + 1) for w in (ffn1_wi, ffn1_wo)] if l + 1 < depth else []
        x, ffn1_w = _ffn(x, (ffn2_g[0], l), *ffn2_w, (ffn2_g[1], l), casts=next_ffn1)
    return x.reshape(bsz, s, d)
```

```python
import functools

import jax
import jax.numpy as jnp
from jax import lax
from jax.experimental import pallas as pl
from jax.experimental.pallas import tpu as pltpu

D_MODEL = 1024
HEAD_DIM = 64
A_GROUPS = 4
A_CHUNK = 128
A_GROUP_DIM = 128
A_WIDTH = A_GROUPS * A_GROUP_DIM
B_PATTERNS = ((128, 1), (512, 4), (2048, 16))
B_GROUPS = len(B_PATTERNS)
B_HEADS_PER_GROUP = 4
B_GROUP_WIDTH = B_HEADS_PER_GROUP * HEAD_DIM
B_WIDTH = B_GROUPS * B_GROUP_WIDTH
B_BLOCK = 128
B_TASKS = 2
C_HEADS = 8
C_WIDTH = C_HEADS * HEAD_DIM
C_BLOCK = 256
C_HEADS_PER_STEP = 8
SPLIT_A = 2 * A_WIDTH
SPLIT_B = SPLIT_A + 3 * B_WIDTH
SPLIT_C = SPLIT_B + 3 * C_WIDTH
D_FF = 2816
EPS = 1e-6
SCALE = HEAD_DIM ** -0.5
LOG2E = 1.4426950408889634
B_Q_SCALE = SCALE * LOG2E
C_Q_SCALE = -SCALE * LOG2E
C_DEAD_LOG2 = -160.0

F32 = jnp.float32
BF16 = jnp.bfloat16

LANES = 128

VMEM_LIMIT_BYTES = 60 * 1024 * 1024

FFN_ROWS = 1024
FFN_COLS = 256
FFN_SUBTILES = 4
PROJ_ROWS = 1024
MERGE_ROWS = 512
MERGE_SUBTILES = 2


def _params(*semantics):
    return pltpu.CompilerParams(dimension_semantics=semantics, vmem_limit_bytes=VMEM_LIMIT_BYTES)


def _rms(x, g):
    return x * lax.rsqrt(jnp.mean(x * x, axis=-1, keepdims=True) + EPS) * g


def _sigmoid(x):
    return 0.5 * jnp.tanh(0.5 * x) + 0.5


def _resident(param):
    stacked, layer = param
    zeros = (0,) * (stacked.ndim - 1)
    return pl.BlockSpec((None, *stacked.shape[1:]), lambda *_: (layer, *zeros), pipeline_mode=pl.Buffered(1))


def _ffn_kernel(x_ref, pre_g_ref, wi_ref, wo_ref, post_g_ref, *refs):
    n_casts = (len(refs) - 2) // 2
    o_ref, act_ref = refs[n_casts], refs[-1]
    for src_ref, dst_ref in zip(refs[:n_casts], refs[n_casts + 1:-1]):
        dst_ref[...] = src_ref[...].astype(BF16)

    sub = FFN_ROWS // FFN_SUBTILES
    ranges = [slice(i * sub, (i + 1) * sub) for i in range(FFN_SUBTILES)]
    hs = [_rms(x_ref[rows, :], pre_g_ref[...]).astype(BF16) for rows in ranges]
    for rows, h in zip(ranges, hs):
        for c in range(D_FF // FFN_COLS):
            gate = jnp.dot(h, wi_ref[:, c * FFN_COLS:(c + 1) * FFN_COLS], preferred_element_type=F32)
            up = jnp.dot(h, wi_ref[:, D_FF + c * FFN_COLS:D_FF + (c + 1) * FFN_COLS], preferred_element_type=F32)
            act_ref[rows, c * FFN_COLS:(c + 1) * FFN_COLS] = (gate * jax.nn.sigmoid(gate) * up).astype(BF16)
    ys = [jnp.dot(act_ref[rows, :], wo_ref[...], preferred_element_type=F32) for rows in ranges]
    for rows, y in zip(ranges, ys):
        o_ref[rows, :] = x_ref[rows, :] + 0.5 * _rms(y, post_g_ref[...])


def _ffn(x, pre_g, wi, wo, post_g, casts=()):
    n = x.shape[0]
    steps = n // FFN_ROWS
    rows = pl.BlockSpec((FFN_ROWS, D_MODEL), lambda i: (i, 0))
    params = (pre_g, wi, wo, post_g)
    cast_in, cast_out, cast_shapes = [], [], []
    for stacked, layer in casts:
        r, c = stacked.shape[1:]
        block = (None, r // steps, c)
        cast_in.append(pl.BlockSpec(block, lambda i, layer=layer: (layer, i, 0)))
        cast_out.append(pl.BlockSpec(block, lambda i: (0, i, 0)))
        cast_shapes.append(jax.ShapeDtypeStruct((1, r, c), BF16))
    out, *converted = pl.pallas_call(
        _ffn_kernel,
        grid=(steps,),
        in_specs=[rows] + [_resident(p) for p in params] + cast_in,
        out_specs=[rows] + cast_out,
        out_shape=[jax.ShapeDtypeStruct((n, D_MODEL), F32)] + cast_shapes,
        scratch_shapes=[pltpu.VMEM((FFN_ROWS, D_FF), BF16)],
        compiler_params=_params("parallel"),
        name="ffn",
    )(x, *(p[0] for p in params), *(p[0] for p in casts))
    return out, [(w, 0) for w in converted]


def _proj_kernel(x_ref, g_ref, w_in_ref, qkvb0_ref, qkvb1_ref, qkvb2_ref, qkvc_ref, class_ref):
    h = _rms(x_ref[...], g_ref[...]).astype(BF16)
    tiles = B_GROUP_WIDTH // LANES
    for g, (out_ref, (_, dil)) in enumerate(zip((qkvb0_ref, qkvb1_ref, qkvb2_ref), B_PATTERNS)):
        for part in range(3):
            first = SPLIT_A + part * B_WIDTH + g * B_GROUP_WIDTH
            y = jnp.dot(h, w_in_ref[:, first:first + B_GROUP_WIDTH], preferred_element_type=F32)
            if part == 0:
                y = y * B_Q_SCALE
            if dil == 1:
                out_ref[:, part * B_GROUP_WIDTH:(part + 1) * B_GROUP_WIDTH] = y.astype(BF16)
            else:
                for c in range(tiles):
                    class_ref[part * tiles + c] = y[:, c * LANES:(c + 1) * LANES]
        if dil > 1:
            width = 3 * B_GROUP_WIDTH
            for r in range(dil):
                for c in range(width // LANES):
                    lanes = slice(r * width + c * LANES, r * width + (c + 1) * LANES)
                    out_ref[:, lanes] = class_ref[c, pl.ds(r, PROJ_ROWS // dil, stride=dil), :].astype(BF16)
    qkvc = jnp.dot(h, w_in_ref[:, SPLIT_B:SPLIT_C], preferred_element_type=F32)
    qkvc_ref[:, :C_WIDTH] = (qkvc[:, :C_WIDTH] * C_Q_SCALE).astype(BF16)
    qkvc_ref[:, C_WIDTH:] = qkvc[:, C_WIDTH:].astype(BF16)


def _proj(x, g, w_in):
    n = x.shape[0]
    rows = lambda w, dil=1: pl.BlockSpec((PROJ_ROWS // dil, dil * w), lambda i: (i, 0))
    widths = [3 * B_GROUP_WIDTH] * B_GROUPS + [3 * C_WIDTH]
    dils = [*(dil for _, dil in B_PATTERNS), 1]
    return pl.pallas_call(
        _proj_kernel,
        grid=(n // PROJ_ROWS,),
        in_specs=[rows(D_MODEL), _resident(g), _resident(w_in)],
        out_specs=[rows(w, dil) for w, dil in zip(widths, dils)],
        out_shape=[jax.ShapeDtypeStruct((n // dil, dil * w), BF16) for w, dil in zip(widths, dils)],
        scratch_shapes=[pltpu.VMEM((3 * B_GROUP_WIDTH // LANES, PROJ_ROWS, LANES), F32)],
        compiler_params=_params("parallel"),
        name="proj",
    )(x, g[0], w_in[0])


def _band_attention_kernel(qkv_ref, o_ref, lse_ref, *, dil):
    length = qkv_ref.shape[0]
    n_blocks = length // B_BLOCK
    window = min(2 * B_BLOCK, length)
    r_idx = lax.broadcasted_iota(jnp.int32, (B_BLOCK, window), 0)
    c_idx = lax.broadcasted_iota(jnp.int32, (B_BLOCK, window), 1)
    first_head = lax.broadcasted_iota(jnp.int32, (B_BLOCK, 2 * HEAD_DIM), 1) < HEAD_DIM
    contract_last = (((1,), (1,)), ((), ()))
    pair_width = 2 * HEAD_DIM
    n_pairs = B_GROUP_WIDTH // pair_width

    def blocks(tasks):
        qs, ks, vs, valids, dests = [], [], [], [], []
        for r, i in tasks:
            base = r * 3 * B_GROUP_WIDTH
            cur = pl.ds(pl.multiple_of(i * B_BLOCK, B_BLOCK), B_BLOCK)
            start = jnp.maximum(i - 1, 0) * B_BLOCK if n_blocks > 1 else 0
            keys = pl.ds(pl.multiple_of(start, B_BLOCK), window)
            ahead = r_idx + (i * B_BLOCK - start) - c_idx
            valid = jnp.logical_and(ahead >= 0, ahead <= B_BLOCK)
            for p in range(n_pairs):
                lanes = lambda part: slice(base + part * B_GROUP_WIDTH + p * pair_width,
                                           base + part * B_GROUP_WIDTH + (p + 1) * pair_width)
                q2 = qkv_ref[cur, lanes(0)]
                zero = jnp.zeros_like(q2)
                qs += [jnp.where(first_head, q2, zero), jnp.where(first_head, zero, q2)]
                ks += [qkv_ref[keys, lanes(1)]] * 2
                vs += [qkv_ref[keys, lanes(2)]] * 2
                valids += [valid] * 2
                dests.append((cur, slice(r * B_GROUP_WIDTH + p * pair_width,
                                         r * B_GROUP_WIDTH + (p + 1) * pair_width)))
        scores = [lax.dot_general(q, k, contract_last, preferred_element_type=F32) for q, k in zip(qs, ks)]
        probs, stats = [], []
        for s, valid in zip(scores, valids):
            s = jnp.where(valid, s, -jnp.inf)
            m = jnp.max(s, axis=1, keepdims=True)
            e = jnp.exp2(s - m)
            l = jnp.sum(e, axis=1, keepdims=True)
            probs.append(e.astype(BF16))
            stats.append((m, l))
        outs = [jnp.dot(p, v, preferred_element_type=F32) for p, v in zip(probs, vs)]
        for d, (rows, lanes) in enumerate(dests):
            (m0, l0), (m1, l1) = stats[2 * d], stats[2 * d + 1]
            l = jnp.where(first_head, l0, l1)
            o_ref[rows, lanes] = jnp.where(first_head, outs[2 * d], outs[2 * d + 1]) / l
            lse_ref[rows, lanes] = jnp.where(first_head, m0, m1) + jnp.log2(l)

    if n_blocks == 1:
        for r in range(0, dil, B_TASKS):
            blocks([(r + t, 0) for t in range(B_TASKS)])
    else:
        for r in range(dil):
            def body(step, carry, r=r):
                blocks([(r, step * B_TASKS + t) for t in range(B_TASKS)])
                return carry
            lax.fori_loop(0, n_blocks // B_TASKS, body, 0)


def _band_attention(qkvb, group, bsz):
    dil = B_PATTERNS[group][1]
    length = qkvb.shape[0] // bsz
    width = qkvb.shape[1] // dil
    view = qkvb.reshape(bsz, length, dil * width)
    out_spec = pl.BlockSpec((None, length, dil * B_GROUP_WIDTH), lambda b: (b, 0, 0))
    out_shape = jax.ShapeDtypeStruct((bsz, length, dil * B_GROUP_WIDTH), F32)
    o, lse = pl.pallas_call(
        functools.partial(_band_attention_kernel, dil=dil),
        grid=(bsz,),
        in_specs=[pl.BlockSpec((None, length, dil * width), lambda b: (b, 0, 0))],
        out_specs=[out_spec, out_spec],
        out_shape=[out_shape, out_shape],
        compiler_params=_params("parallel"),
        name=f"band_attention_g{group}",
    )(view)
    return o.reshape(bsz * length, dil * B_GROUP_WIDTH), lse.reshape(bsz * length, dil * B_GROUP_WIDTH)


def _stick_breaking_tiles(qs, ks, vs, upper, mask):
    contract_last = (((1,), (1,)), ((), ()))
    zs = [lax.dot_general(q, k, contract_last, preferred_element_type=F32) for q, k in zip(qs, ks)]
    log_keeps, logits, afters = [], [], []
    for nw in zs:
        log_keep = jnp.minimum(nw, 0.0) - jnp.log2(1.0 + jnp.exp2(-jnp.abs(nw)))
        if mask is not None:
            log_keep = jnp.where(mask, log_keep, 0.0)
        afters.append(jnp.dot(log_keep.astype(BF16), upper, preferred_element_type=F32))
        log_keeps.append(log_keep)
        logits.append(log_keep - nw)
    out = []
    for log_keep, logit, after, v in zip(log_keeps, logits, afters, vs):
        a = jnp.exp2(logit + after)
        if mask is not None:
            a = jnp.where(mask, a, 0.0)
        contrib = jnp.dot(a.astype(BF16), v, preferred_element_type=F32)
        out.append((contrib, jnp.sum(log_keep, axis=1, keepdims=True)))
    return out


def _stick_breaking_kernel(q_ref, k_ref, v_ref, o_ref, acc_ref, passed_ref):
    n_blocks = q_ref.shape[0] // C_BLOCK
    row = lax.broadcasted_iota(jnp.int32, (C_BLOCK, C_BLOCK), 0)
    col = lax.broadcasted_iota(jnp.int32, (C_BLOCK, C_BLOCK), 1)
    strictly_lower = col < row
    upper = jnp.where(row > col, 1.0, 0.0).astype(BF16)
    pair_width = 2 * HEAD_DIM
    pairs = [slice(p * pair_width, (p + 1) * pair_width) for p in range(C_HEADS_PER_STEP // 2)]
    first_head = lax.broadcasted_iota(jnp.int32, (C_BLOCK, pair_width), 1) < HEAD_DIM

    def q_block(i, carry):
        rows_i = pl.ds(pl.multiple_of(i * C_BLOCK, C_BLOCK), C_BLOCK)

        def tiles(rows_j, mask):
            qs, ks, vs = [], [], []
            for lanes in pairs:
                q2 = q_ref[rows_i, lanes]
                zero = jnp.zeros_like(q2)
                qs += [jnp.where(first_head, q2, zero), jnp.where(first_head, zero, q2)]
                ks += [k_ref[rows_j, lanes]] * 2
                vs += [v_ref[rows_j, lanes]] * 2
            res = _stick_breaking_tiles(qs, ks, vs, upper, mask)
            return [(jnp.where(first_head, res[2 * p][0], res[2 * p + 1][0]),
                     jnp.where(first_head, res[2 * p][1], res[2 * p + 1][1])) for p in range(len(pairs))]

        for p, (contrib, row_sum) in enumerate(tiles(rows_i, strictly_lower)):
            acc_ref[p] = contrib
            passed_ref[p] = row_sum

        def k_block(state):
            step, _ = state
            rows_j = pl.ds(pl.multiple_of((i - 1 - step) * C_BLOCK, C_BLOCK), C_BLOCK)
            most_alive = None
            for p, (contrib, row_sum) in enumerate(tiles(rows_j, None)):
                passed = passed_ref[p]
                acc_ref[p] += contrib * jnp.exp2(passed)
                passed = passed + row_sum
                passed_ref[p] = passed
                most_alive = passed if most_alive is None else jnp.maximum(most_alive, passed)
            return step + 1, jnp.max(most_alive)

        def more_to_do(state):
            step, most_alive = state
            return jnp.logical_and(step < i, most_alive > C_DEAD_LOG2)

        lax.while_loop(more_to_do, k_block, (jnp.int32(0), jnp.float32(0.0)))
        for p, lanes in enumerate(pairs):
            o_ref[rows_i, lanes] = acc_ref[p].astype(o_ref.dtype)
        return carry

    lax.fori_loop(0, n_blocks, q_block, 0)


def _stick_breaking(qkvc):
    bsz, s, _ = qkvc.shape
    width = C_HEADS_PER_STEP * HEAD_DIM
    n_steps = C_WIDTH // width

    def spec(part):
        return pl.BlockSpec((None, s, width), lambda b, p: (b, 0, part * n_steps + p))

    return pl.pallas_call(
        _stick_breaking_kernel,
        grid=(bsz, n_steps),
        in_specs=[spec(0), spec(1), spec(2)],
        out_specs=pl.BlockSpec((None, s, width), lambda b, p: (b, 0, p)),
        out_shape=jax.ShapeDtypeStruct((bsz, s, C_WIDTH), BF16),
        scratch_shapes=[pltpu.VMEM((C_HEADS_PER_STEP // 2, C_BLOCK, 2 * HEAD_DIM), F32),
                        pltpu.VMEM((C_HEADS_PER_STEP // 2, C_BLOCK, 2 * HEAD_DIM), F32)],
        compiler_params=_params("parallel", "parallel"),
        name="stick_breaking",
    )(qkvc, qkvc, qkvc)


def _fill_token_order(src_ref, scratch_ref, dil):
    tiles, rows, _ = scratch_ref.shape
    for r in range(dil):
        for c in range(tiles):
            lanes = slice((r * tiles + c) * LANES, (r * tiles + c + 1) * LANES)
            scratch_ref[c, pl.ds(r, rows // dil, stride=dil), :] = src_ref[:, lanes]


def _merge_kernel(x_ref, o0_ref, l0_ref, o1_ref, l1_ref, o2_ref, l2_ref, yc_ref,
                  pre_g_ref, w_in_ref, ln_g_ref, ln_b_ref, ws_ref, bs_ref,
                  wpa_ref, wpb_ref, wpc_ref, wo_ref, post_g_ref, out_ref, *order_refs):
    sub = MERGE_ROWS // MERGE_SUBTILES
    ranges = [slice(i * sub, (i + 1) * sub) for i in range(MERGE_SUBTILES)]
    dot = functools.partial(jnp.dot, preferred_element_type=F32)

    b_srcs = (o0_ref, l0_ref, o1_ref, l1_ref, o2_ref, l2_ref)
    b_dils = [dil for _, dil in B_PATTERNS for _ in range(2)]
    for src, scratch, dil in zip(b_srcs, order_refs, b_dils):
        if dil > 1:
            _fill_token_order(src, scratch, dil)

    def b_part(k, rows):
        if b_dils[k] == 1:
            return b_srcs[k][rows, :]
        return jnp.concatenate([order_refs[k][c, rows, :] for c in range(order_refs[k].shape[0])], axis=1)

    hs = [_rms(x_ref[rows, :], pre_g_ref[...]).astype(BF16) for rows in ranges]
    gate = lambda h, b: dot(h, w_in_ref[:, SPLIT_C + b * D_MODEL:SPLIT_C + (b + 1) * D_MODEL])
    zas = [dot(h, w_in_ref[:, :SPLIT_A]) for h in hs]

    p_cs = [dot(yc_ref[rows, :], wpc_ref[...]) for rows in ranges]
    mergeds = [_sigmoid(gate(h, 2)) * p_c for h, p_c in zip(hs, p_cs)]

    y_bs = []
    for rows in ranges:
        o0, l0, o1, l1, o2, l2 = [b_part(k, rows) for k in range(6)]
        m = jnp.maximum(jnp.maximum(l0, l1), l2)
        w0, w1, w2 = jnp.exp2(l0 - m), jnp.exp2(l1 - m), jnp.exp2(l2 - m)
        y_bs.append(((w0 * o0 + w1 * o1 + w2 * o2) / (w0 + w1 + w2)).astype(BF16))
    p_bs = [dot(y_b, wpb_ref[...]) for y_b in y_bs]
    mergeds = [merged + _sigmoid(gate(h, 1)) * p_b for merged, h, p_b in zip(mergeds, hs, p_bs)]

    us, vs = [], []
    for za in zas:
        z = jax.nn.gelu(za, approximate=True)
        v = z[:, A_WIDTH:]
        vc = v - jnp.mean(v, axis=-1, keepdims=True)
        v = vc * lax.rsqrt(jnp.mean(vc * vc, axis=-1, keepdims=True) + EPS) * ln_g_ref[...] + ln_b_ref[...]
        us.append(z[:, :A_WIDTH])
        vs.append(v.astype(BF16))
    t_idx = lax.broadcasted_iota(jnp.int32, (A_CHUNK, A_CHUNK), 0)
    s_idx = lax.broadcasted_iota(jnp.int32, (A_CHUNK, A_CHUNK), 1)
    ws = [jnp.where(s_idx <= t_idx, ws_ref[g], 0.0).astype(BF16) for g in range(A_GROUPS)]
    n_chunks = sub // A_CHUNK
    y_as = []
    for u, v in zip(us, vs):
        mixed = []
        for g in range(A_GROUPS):
            vg = jnp.concatenate([v[c * A_CHUNK:(c + 1) * A_CHUNK, g * A_GROUP_DIM:(g + 1) * A_GROUP_DIM]
                                  for c in range(n_chunks)], axis=1)
            mixed.append(dot(ws[g], vg))
        sv = jnp.concatenate(
            [jnp.concatenate([m[:, c * A_GROUP_DIM:(c + 1) * A_GROUP_DIM] for m in mixed], axis=1) + bs_ref[...]
             for c in range(n_chunks)], axis=0)
        y_as.append((u * sv).astype(BF16))
    p_as = [dot(y_a, wpa_ref[...]) for y_a in y_as]
    mergeds = [merged + _sigmoid(gate(h, 0)) * p_a for merged, h, p_a in zip(mergeds, hs, p_as)]

    ys = [dot(merged.astype(BF16), wo_ref[...]) for merged in mergeds]
    for rows, y in zip(ranges, ys):
        out_ref[rows, :] = x_ref[rows, :] + _rms(y, post_g_ref[...])


def _merge(x, b_parts, yc, *params):
    n = x.shape[0]
    rows = lambda w: pl.BlockSpec((MERGE_ROWS, w), lambda i: (i, 0))
    b_flat = [a for part in b_parts for a in part]
    b_specs = [pl.BlockSpec((MERGE_ROWS // dil, dil * B_GROUP_WIDTH), lambda i: (i, 0))
               for _, dil in B_PATTERNS for _ in range(2)]
    return pl.pallas_call(
        _merge_kernel,
        grid=(n // MERGE_ROWS,),
        in_specs=[rows(D_MODEL)] + b_specs + [rows(C_WIDTH)] + [_resident(p) for p in params],
        out_specs=rows(D_MODEL),
        out_shape=jax.ShapeDtypeStruct((n, D_MODEL), F32),
        scratch_shapes=[pltpu.VMEM((B_GROUP_WIDTH // LANES, MERGE_ROWS, LANES), F32)] * (2 * B_GROUPS),
        compiler_params=_params("parallel"),
        name="merge",
    )(x, *b_flat, yc, *(p[0] for p in params))


def kernel(x, ffn1_pre_g, ffn1_wi, ffn1_wo, ffn1_post_g, mix_pre_g, w_in, a_ln_g, a_ln_b, a_ws, a_bs,
           w_pa, w_pb, w_pc, w_o, mix_post_g, ffn2_pre_g, ffn2_wi, ffn2_wo, ffn2_post_g):
    bsz, s, d = x.shape
    n = bsz * s
    depth = ffn1_wi.shape[0]
    rows = lambda p: p[:, None, :]
    ffn1_g, ffn2_g = (rows(ffn1_pre_g), rows(ffn1_post_g)), (rows(ffn2_pre_g), rows(ffn2_post_g))
    mix_pre_g, mix_post_g, a_ln_g, a_ln_b = rows(mix_pre_g), rows(mix_post_g), rows(a_ln_g), rows(a_ln_b)
    bs_full = jnp.repeat(jnp.swapaxes(a_bs, 1, 2), A_GROUP_DIM, axis=2)
    mixer_weights = (w_in, w_pa, w_pb, w_pc, w_o)
    ffn1_w = [(ffn1_wi[:1].astype(BF16), 0), (ffn1_wo[:1].astype(BF16), 0)]

    x = x.reshape(n, d)
    for l in range(depth):
        x, converted = _ffn(x, (ffn1_g[0], l), *ffn1_w, (ffn1_g[1], l),
                            casts=[(w, l) for w in (*mixer_weights, ffn2_wi, ffn2_wo)])
        (w_in_l, w_pa_l, w_pb_l, w_pc_l, w_o_l), ffn2_w = converted[:5], converted[5:]
        *qkvbs, qkvc = _proj(x, (mix_pre_g, l), w_in_l)
        b_parts = [_band_attention(qkvb, g, bsz) for g, qkvb in enumerate(qkvbs)]
        yc = _stick_breaking(qkvc.reshape(bsz, s, 3 * C_WIDTH)).reshape(n, C_WIDTH)
        x = _merge(x, b_parts, yc, (mix_pre_g, l), w_in_l, (a_ln_g, l), (a_ln_b, l), (a_ws, l), (bs_full, l),
                   w_pa_l, w_pb_l, w_pc_l, w_o_l, (mix_post_g, l))
        next_ffn1 = [(w, l + 1) for w in (ffn1_wi, ffn1_wo)] if l + 1 < depth else []
        x, ffn1_w = _ffn(x, (ffn2_g[0], l), *ffn2_w, (ffn2_g[1], l), casts=next_ffn1)
    return x.reshape(bsz, s, d)
```

```python
import functools

import jax
import jax.numpy as jnp
from jax import lax
from jax.experimental import pallas as pl
from jax.experimental.pallas import tpu as pltpu

D_MODEL = 1024
HEAD_DIM = 64
A_GROUPS = 4
A_CHUNK = 128
A_GROUP_DIM = 128
A_WIDTH = A_GROUPS * A_GROUP_DIM
B_PATTERNS = ((128, 1), (512, 4), (2048, 16))
B_GROUPS = len(B_PATTERNS)
B_HEADS_PER_GROUP = 4
B_GROUP_WIDTH = B_HEADS_PER_GROUP * HEAD_DIM
B_WIDTH = B_GROUPS * B_GROUP_WIDTH
B_BLOCK = 128
B_TASKS = 2
C_HEADS = 8
C_WIDTH = C_HEADS * HEAD_DIM
C_BLOCK = 256
C_HEADS_PER_STEP = 8
SPLIT_A = 2 * A_WIDTH
SPLIT_B = SPLIT_A + 3 * B_WIDTH
SPLIT_C = SPLIT_B + 3 * C_WIDTH
D_FF = 2816
EPS = 1e-6
SCALE = HEAD_DIM ** -0.5
LOG2E = 1.4426950408889634
B_Q_SCALE = SCALE * LOG2E
C_Q_SCALE = -SCALE * LOG2E
C_DEAD_LOG2 = -160.0

F32 = jnp.float32
BF16 = jnp.bfloat16

LANES = 128

V7X_VMEM_BYTES = 64 * 1024 * 1024
VMEM_LIMIT_BYTES = V7X_VMEM_BYTES - 4 * 1024 * 1024

FFN_ROWS = 1024
FFN_COLS = 256
FFN_SUBTILES = 4
PROJ_ROWS = 1024
MERGE_ROWS = 512
MERGE_SUBTILES = 2


def _params(*semantics):
    return pltpu.CompilerParams(dimension_semantics=semantics, vmem_limit_bytes=VMEM_LIMIT_BYTES)


def _rms(x, g):
    return x * lax.rsqrt(jnp.mean(x * x, axis=-1, keepdims=True) + EPS) * g


def _sigmoid(x):
    return 0.5 * jnp.tanh(0.5 * x) + 0.5


def _resident(param):
    stacked, layer = param
    zeros = (0,) * (stacked.ndim - 1)
    return pl.BlockSpec((None, *stacked.shape[1:]), lambda *_: (layer, *zeros), pipeline_mode=pl.Buffered(1))


def _ffn_kernel(x_ref, pre_g_ref, wi_ref, wo_ref, post_g_ref, *refs):
    n_casts = (len(refs) - 2) // 2
    o_ref, act_ref = refs[n_casts], refs[-1]
    for src_ref, dst_ref in zip(refs[:n_casts], refs[n_casts + 1:-1]):
        dst_ref[...] = src_ref[...].astype(BF16)

    sub = FFN_ROWS // FFN_SUBTILES
    ranges = [slice(i * sub, (i + 1) * sub) for i in range(FFN_SUBTILES)]
    hs = [_rms(x_ref[rows, :], pre_g_ref[...]).astype(BF16) for rows in ranges]
    for rows, h in zip(ranges, hs):
        for c in range(D_FF // FFN_COLS):
            gate = jnp.dot(h, wi_ref[:, c * FFN_COLS:(c + 1) * FFN_COLS], preferred_element_type=F32)
            up = jnp.dot(h, wi_ref[:, D_FF + c * FFN_COLS:D_FF + (c + 1) * FFN_COLS], preferred_element_type=F32)
            act_ref[rows, c * FFN_COLS:(c + 1) * FFN_COLS] = (gate * _sigmoid(gate) * up).astype(BF16)
    ys = [jnp.dot(act_ref[rows, :], wo_ref[...], preferred_element_type=F32) for rows in ranges]
    for rows, y in zip(ranges, ys):
        o_ref[rows, :] = x_ref[rows, :] + 0.5 * _rms(y, post_g_ref[...])


def _ffn(x, pre_g, wi, wo, post_g, casts=()):
    n = x.shape[0]
    steps = n // FFN_ROWS
    rows = pl.BlockSpec((FFN_ROWS, D_MODEL), lambda i: (i, 0))
    params = (pre_g, wi, wo, post_g)
    cast_in, cast_out, cast_shapes = [], [], []
    for stacked, layer in casts:
        r, c = stacked.shape[1:]
        block = (None, r // steps, c)
        cast_in.append(pl.BlockSpec(block, lambda i, layer=layer: (layer, i, 0)))
        cast_out.append(pl.BlockSpec(block, lambda i: (0, i, 0)))
        cast_shapes.append(jax.ShapeDtypeStruct((1, r, c), BF16))
    out, *converted = pl.pallas_call(
        _ffn_kernel,
        grid=(steps,),
        in_specs=[rows] + [_resident(p) for p in params] + cast_in,
        out_specs=[rows] + cast_out,
        out_shape=[jax.ShapeDtypeStruct((n, D_MODEL), F32)] + cast_shapes,
        scratch_shapes=[pltpu.VMEM((FFN_ROWS, D_FF), BF16)],
        compiler_params=_params("parallel"),
        name="ffn",
    )(x, *(p[0] for p in params), *(p[0] for p in casts))
    return out, [(w, 0) for w in converted]


def _proj_kernel(x_ref, g_ref, w_in_ref, qkvb0_ref, qkvb1_ref, qkvb2_ref, qkvc_ref, class_ref):
    h = _rms(x_ref[...], g_ref[...]).astype(BF16)
    tiles = B_GROUP_WIDTH // LANES
    for g, (out_ref, (_, dil)) in enumerate(zip((qkvb0_ref, qkvb1_ref, qkvb2_ref), B_PATTERNS)):
        for part in range(3):
            first = SPLIT_A + part * B_WIDTH + g * B_GROUP_WIDTH
            y = jnp.dot(h, w_in_ref[:, first:first + B_GROUP_WIDTH], preferred_element_type=F32)
            if part == 0:
                y = y * B_Q_SCALE
            if dil == 1:
                out_ref[:, part * B_GROUP_WIDTH:(part + 1) * B_GROUP_WIDTH] = y.astype(BF16)
            else:
                for c in range(tiles):
                    class_ref[part * tiles + c] = y[:, c * LANES:(c + 1) * LANES]
        if dil > 1:
            width = 3 * B_GROUP_WIDTH
            for r in range(dil):
                for c in range(width // LANES):
                    lanes = slice(r * width + c * LANES, r * width + (c + 1) * LANES)
                    out_ref[:, lanes] = class_ref[c, pl.ds(r, PROJ_ROWS // dil, stride=dil), :].astype(BF16)
    qkvc = jnp.dot(h, w_in_ref[:, SPLIT_B:SPLIT_C], preferred_element_type=F32)
    qkvc_ref[:, :C_WIDTH] = (qkvc[:, :C_WIDTH] * C_Q_SCALE).astype(BF16)
    qkvc_ref[:, C_WIDTH:] = qkvc[:, C_WIDTH:].astype(BF16)


def _proj(x, g, w_in):
    n = x.shape[0]
    rows = lambda w, dil=1: pl.BlockSpec((PROJ_ROWS // dil, dil * w), lambda i: (i, 0))
    widths = [3 * B_GROUP_WIDTH] * B_GROUPS + [3 * C_WIDTH]
    dils = [*(dil for _, dil in B_PATTERNS), 1]
    return pl.pallas_call(
        _proj_kernel,
        grid=(n // PROJ_ROWS,),
        in_specs=[rows(D_MODEL), _resident(g), _resident(w_in)],
        out_specs=[rows(w, dil) for w, dil in zip(widths, dils)],
        out_shape=[jax.ShapeDtypeStruct((n // dil, dil * w), BF16) for w, dil in zip(widths, dils)],
        scratch_shapes=[pltpu.VMEM((3 * B_GROUP_WIDTH // LANES, PROJ_ROWS, LANES), F32)],
        compiler_params=_params("parallel"),
        name="proj",
    )(x, g[0], w_in[0])


def _band_attention_kernel(qkv_ref, o_ref, lse_ref, *, dil):
    length = qkv_ref.shape[0]
    n_blocks = length // B_BLOCK
    window = min(2 * B_BLOCK, length)
    r_idx = lax.broadcasted_iota(jnp.int32, (B_BLOCK, window), 0)
    c_idx = lax.broadcasted_iota(jnp.int32, (B_BLOCK, window), 1)
    first_head = lax.broadcasted_iota(jnp.int32, (B_BLOCK, 2 * HEAD_DIM), 1) < HEAD_DIM
    contract_last = (((1,), (1,)), ((), ()))
    pair_width = 2 * HEAD_DIM
    n_pairs = B_GROUP_WIDTH // pair_width

    def blocks(tasks):
        qs, ks, vs, valids, dests = [], [], [], [], []
        for r, i in tasks:
            base = r * 3 * B_GROUP_WIDTH
            cur = pl.ds(pl.multiple_of(i * B_BLOCK, B_BLOCK), B_BLOCK)
            start = jnp.maximum(i - 1, 0) * B_BLOCK if n_blocks > 1 else 0
            keys = pl.ds(pl.multiple_of(start, B_BLOCK), window)
            ahead = r_idx + (i * B_BLOCK - start) - c_idx
            valid = jnp.logical_and(ahead >= 0, ahead <= B_BLOCK)
            for p in range(n_pairs):
                lanes = lambda part: slice(base + part * B_GROUP_WIDTH + p * pair_width,
                                           base + part * B_GROUP_WIDTH + (p + 1) * pair_width)
                q2 = qkv_ref[cur, lanes(0)]
                zero = jnp.zeros_like(q2)
                qs += [jnp.where(first_head, q2, zero), jnp.where(first_head, zero, q2)]
                ks += [qkv_ref[keys, lanes(1)]] * 2
                vs += [qkv_ref[keys, lanes(2)]] * 2
                valids += [valid] * 2
                dests.append((cur, slice(r * B_GROUP_WIDTH + p * pair_width,
                                         r * B_GROUP_WIDTH + (p + 1) * pair_width)))
        scores = [lax.dot_general(q, k, contract_last, preferred_element_type=F32) for q, k in zip(qs, ks)]
        probs, stats = [], []
        for s, valid in zip(scores, valids):
            s = jnp.where(valid, s, -jnp.inf)
            m = jnp.max(s, axis=1, keepdims=True)
            e = jnp.exp2(s - m)
            l = jnp.sum(e, axis=1, keepdims=True)
            probs.append(e.astype(BF16))
            stats.append((m, l))
        outs = [jnp.dot(p, v, preferred_element_type=F32) for p, v in zip(probs, vs)]
        for d, (rows, lanes) in enumerate(dests):
            (m0, l0), (m1, l1) = stats[2 * d], stats[2 * d + 1]
            l = jnp.where(first_head, l0, l1)
            o_ref[rows, lanes] = jnp.where(first_head, outs[2 * d], outs[2 * d + 1]) / l
            lse_ref[rows, lanes] = jnp.where(first_head, m0, m1) + jnp.log2(l)

    if n_blocks == 1:
        for r in range(0, dil, B_TASKS):
            blocks([(r + t, 0) for t in range(B_TASKS)])
    else:
        for r in range(dil):
            def body(step, carry, r=r):
                blocks([(r, step * B_TASKS + t) for t in range(B_TASKS)])
                return carry
            lax.fori_loop(0, n_blocks // B_TASKS, body, 0)


def _band_attention(qkvb, group, bsz):
    dil = B_PATTERNS[group][1]
    length = qkvb.shape[0] // bsz
    width = qkvb.shape[1] // dil
    view = qkvb.reshape(bsz, length, dil * width)
    out_spec = pl.BlockSpec((None, length, dil * B_GROUP_WIDTH), lambda b: (b, 0, 0))
    out_shape = jax.ShapeDtypeStruct((bsz, length, dil * B_GROUP_WIDTH), F32)
    o, lse = pl.pallas_call(
        functools.partial(_band_attention_kernel, dil=dil),
        grid=(bsz,),
        in_specs=[pl.BlockSpec((None, length, dil * width), lambda b: (b, 0, 0))],
        out_specs=[out_spec, out_spec],
        out_shape=[out_shape, out_shape],
        compiler_params=_params("parallel"),
        name=f"band_attention_g{group}",
    )(view)
    return o.reshape(bsz * length, dil * B_GROUP_WIDTH), lse.reshape(bsz * length, dil * B_GROUP_WIDTH)


def _stick_breaking_tiles(qs, ks, vs, upper, mask):
    contract_last = (((1,), (1,)), ((), ()))
    zs = [lax.dot_general(q, k, contract_last, preferred_element_type=F32) for q, k in zip(qs, ks)]
    log_keeps, logits, afters = [], [], []
    for nw in zs:
        log_keep = jnp.minimum(nw, 0.0) - jnp.log2(1.0 + jnp.exp2(-jnp.abs(nw)))
        if mask is not None:
            log_keep = jnp.where(mask, log_keep, 0.0)
        afters.append(jnp.dot(log_keep.astype(BF16), upper, preferred_element_type=F32))
        log_keeps.append(log_keep)
        logits.append(log_keep - nw)
    out = []
    for log_keep, logit, after, v in zip(log_keeps, logits, afters, vs):
        a = jnp.exp2(logit + after)
        if mask is not None:
            a = jnp.where(mask, a, 0.0)
        contrib = jnp.dot(a.astype(BF16), v, preferred_element_type=F32)
        out.append((contrib, jnp.sum(log_keep, axis=1, keepdims=True)))
    return out


def _stick_breaking_kernel(q_ref, k_ref, v_ref, o_ref, acc_ref, passed_ref):
    n_blocks = q_ref.shape[0] // C_BLOCK
    row = lax.broadcasted_iota(jnp.int32, (C_BLOCK, C_BLOCK), 0)
    col = lax.broadcasted_iota(jnp.int32, (C_BLOCK, C_BLOCK), 1)
    strictly_lower = col < row
    upper = jnp.where(row > col, 1.0, 0.0).astype(BF16)
    pair_width = 2 * HEAD_DIM
    pairs = [slice(p * pair_width, (p + 1) * pair_width) for p in range(C_HEADS_PER_STEP // 2)]
    first_head = lax.broadcasted_iota(jnp.int32, (C_BLOCK, pair_width), 1) < HEAD_DIM

    def q_block(i, carry):
        rows_i = pl.ds(pl.multiple_of(i * C_BLOCK, C_BLOCK), C_BLOCK)

        def tiles(rows_j, mask):
            qs, ks, vs = [], [], []
            for lanes in pairs:
                q2 = q_ref[rows_i, lanes]
                zero = jnp.zeros_like(q2)
                qs += [jnp.where(first_head, q2, zero), jnp.where(first_head, zero, q2)]
                ks += [k_ref[rows_j, lanes]] * 2
                vs += [v_ref[rows_j, lanes]] * 2
            res = _stick_breaking_tiles(qs, ks, vs, upper, mask)
            return [(jnp.where(first_head, res[2 * p][0], res[2 * p + 1][0]),
                     jnp.where(first_head, res[2 * p][1], res[2 * p + 1][1])) for p in range(len(pairs))]

        for p, (contrib, row_sum) in enumerate(tiles(rows_i, strictly_lower)):
            acc_ref[p] = contrib
            passed_ref[p] = row_sum

        def k_block(state):
            step, _ = state
            rows_j = pl.ds(pl.multiple_of((i - 1 - step) * C_BLOCK, C_BLOCK), C_BLOCK)
            most_alive = None
            for p, (contrib, row_sum) in enumerate(tiles(rows_j, None)):
                passed = passed_ref[p]
                acc_ref[p] += contrib * jnp.exp2(passed)
                passed = passed + row_sum
                passed_ref[p] = passed
                most_alive = passed if most_alive is None else jnp.maximum(most_alive, passed)
            return step + 1, jnp.max(most_alive)

        def more_to_do(state):
            step, most_alive = state
            return jnp.logical_and(step < i, most_alive > C_DEAD_LOG2)

        lax.while_loop(more_to_do, k_block, (jnp.int32(0), jnp.float32(0.0)))
        for p, lanes in enumerate(pairs):
            o_ref[rows_i, lanes] = acc_ref[p].astype(o_ref.dtype)
        return carry

    lax.fori_loop(0, n_blocks, q_block, 0)


def _stick_breaking(qkvc):
    bsz, s, _ = qkvc.shape
    width = C_HEADS_PER_STEP * HEAD_DIM
    n_steps = C_WIDTH // width

    def spec(part):
        return pl.BlockSpec((None, s, width), lambda b, p: (b, 0, part * n_steps + p))

    return pl.pallas_call(
        _stick_breaking_kernel,
        grid=(bsz, n_steps),
        in_specs=[spec(0), spec(1), spec(2)],
        out_specs=pl.BlockSpec((None, s, width), lambda b, p: (b, 0, p)),
        out_shape=jax.ShapeDtypeStruct((bsz, s, C_WIDTH), BF16),
        scratch_shapes=[pltpu.VMEM((C_HEADS_PER_STEP // 2, C_BLOCK, 2 * HEAD_DIM), F32),
                        pltpu.VMEM((C_HEADS_PER_STEP // 2, C_BLOCK, 2 * HEAD_DIM), F32)],
        compiler_params=_params("parallel", "parallel"),
        name="stick_breaking",
    )(qkvc, qkvc, qkvc)


def _fill_token_order(src_ref, scratch_ref, dil):
    tiles, rows, _ = scratch_ref.shape
    for r in range(dil):
        for c in range(tiles):
            lanes = slice((r * tiles + c) * LANES, (r * tiles + c + 1) * LANES)
            scratch_ref[c, pl.ds(r, rows // dil, stride=dil), :] = src_ref[:, lanes]


def _merge_kernel(x_ref, o0_ref, l0_ref, o1_ref, l1_ref, o2_ref, l2_ref, yc_ref,
                  pre_g_ref, w_in_ref, ln_g_ref, ln_b_ref, ws_ref, bs_ref,
                  wpa_ref, wpb_ref, wpc_ref, wo_ref, post_g_ref, out_ref, *order_refs):
    sub = MERGE_ROWS // MERGE_SUBTILES
    ranges = [slice(i * sub, (i + 1) * sub) for i in range(MERGE_SUBTILES)]
    dot = functools.partial(jnp.dot, preferred_element_type=F32)

    b_srcs = (o0_ref, l0_ref, o1_ref, l1_ref, o2_ref, l2_ref)
    b_dils = [dil for _, dil in B_PATTERNS for _ in range(2)]
    for src, scratch, dil in zip(b_srcs, order_refs, b_dils):
        if dil > 1:
            _fill_token_order(src, scratch, dil)

    def b_part(k, rows):
        if b_dils[k] == 1:
            return b_srcs[k][rows, :]
        return jnp.concatenate([order_refs[k][c, rows, :] for c in range(order_refs[k].shape[0])], axis=1)

    hs = [_rms(x_ref[rows, :], pre_g_ref[...]).astype(BF16) for rows in ranges]
    gate = lambda h, b: dot(h, w_in_ref[:, SPLIT_C + b * D_MODEL:SPLIT_C + (b + 1) * D_MODEL])
    zas = [dot(h, w_in_ref[:, :SPLIT_A]) for h in hs]

    p_cs = [dot(yc_ref[rows, :], wpc_ref[...]) for rows in ranges]
    mergeds = [_sigmoid(gate(h, 2)) * p_c for h, p_c in zip(hs, p_cs)]

    y_bs = []
    for rows in ranges:
        o0, l0, o1, l1, o2, l2 = [b_part(k, rows) for k in range(6)]
        m = jnp.maximum(jnp.maximum(l0, l1), l2)
        w0, w1, w2 = jnp.exp2(l0 - m), jnp.exp2(l1 - m), jnp.exp2(l2 - m)
        y_bs.append(((w0 * o0 + w1 * o1 + w2 * o2) / (w0 + w1 + w2)).astype(BF16))
    p_bs = [dot(y_b, wpb_ref[...]) for y_b in y_bs]
    mergeds = [merged + _sigmoid(gate(h, 1)) * p_b for merged, h, p_b in zip(mergeds, hs, p_bs)]

    us, vs = [], []
    for za in zas:
        z = jax.nn.gelu(za, approximate=True)
        v = z[:, A_WIDTH:]
        vc = v - jnp.mean(v, axis=-1, keepdims=True)
        v = vc * lax.rsqrt(jnp.mean(vc * vc, axis=-1, keepdims=True) + EPS) * ln_g_ref[...] + ln_b_ref[...]
        us.append(z[:, :A_WIDTH])
        vs.append(v.astype(BF16))
    t_idx = lax.broadcasted_iota(jnp.int32, (A_CHUNK, A_CHUNK), 0)
    s_idx = lax.broadcasted_iota(jnp.int32, (A_CHUNK, A_CHUNK), 1)
    ws = [jnp.where(s_idx <= t_idx, ws_ref[g], 0.0).astype(BF16) for g in range(A_GROUPS)]
    n_chunks = sub // A_CHUNK
    y_as = []
    for u, v in zip(us, vs):
        mixed = []
        for g in range(A_GROUPS):
            vg = jnp.concatenate([v[c * A_CHUNK:(c + 1) * A_CHUNK, g * A_GROUP_DIM:(g + 1) * A_GROUP_DIM]
                                  for c in range(n_chunks)], axis=1)
            mixed.append(dot(ws[g], vg))
        sv = jnp.concatenate(
            [jnp.concatenate([m[:, c * A_GROUP_DIM:(c + 1) * A_GROUP_DIM] for m in mixed], axis=1) + bs_ref[...]
             for c in range(n_chunks)], axis=0)
        y_as.append((u * sv).astype(BF16))
    p_as = [dot(y_a, wpa_ref[...]) for y_a in y_as]
    mergeds = [merged + _sigmoid(gate(h, 0)) * p_a for merged, h, p_a in zip(mergeds, hs, p_as)]

    ys = [dot(merged.astype(BF16), wo_ref[...]) for merged in mergeds]
    for rows, y in zip(ranges, ys):
        out_ref[rows, :] = x_ref[rows, :] + _rms(y, post_g_ref[...])


def _merge(x, b_parts, yc, *params):
    n = x.shape[0]
    rows = lambda w: pl.BlockSpec((MERGE_ROWS, w), lambda i: (i, 0))
    b_flat = [a for part in b_parts for a in part]
    b_specs = [pl.BlockSpec((MERGE_ROWS // dil, dil * B_GROUP_WIDTH), lambda i: (i, 0))
               for _, dil in B_PATTERNS for _ in range(2)]
    return pl.pallas_call(
        _merge_kernel,
        grid=(n // MERGE_ROWS,),
        in_specs=[rows(D_MODEL)] + b_specs + [rows(C_WIDTH)] + [_resident(p) for p in params],
        out_specs=rows(D_MODEL),
        out_shape=jax.ShapeDtypeStruct((n, D_MODEL), F32),
        scratch_shapes=[pltpu.VMEM((B_GROUP_WIDTH // LANES, MERGE_ROWS, LANES), F32)] * (2 * B_GROUPS),
        compiler_params=_params("parallel"),
        name="merge",
    )(x, *b_flat, yc, *(p[0] for p in params))


def kernel(x, ffn1_pre_g, ffn1_wi, ffn1_wo, ffn1_post_g, mix_pre_g, w_in, a_ln_g, a_ln_b, a_ws, a_bs,
           w_pa, w_pb, w_pc, w_o, mix_post_g, ffn2_pre_g, ffn2_wi, ffn2_wo, ffn2_post_g):
    bsz, s, d = x.shape
    n = bsz * s
    depth = ffn1_wi.shape[0]
    rows = lambda p: p[:, None, :]
    ffn1_g, ffn2_g = (rows(ffn1_pre_g), rows(ffn1_post_g)), (rows(ffn2_pre_g), rows(ffn2_post_g))
    mix_pre_g, mix_post_g, a_ln_g, a_ln_b = rows(mix_pre_g), rows(mix_post_g), rows(a_ln_g), rows(a_ln_b)
    bs_full = jnp.repeat(jnp.swapaxes(a_bs, 1, 2), A_GROUP_DIM, axis=2)
    mixer_weights = (w_in, w_pa, w_pb, w_pc, w_o)
    ffn1_w = [(ffn1_wi[:1].astype(BF16), 0), (ffn1_wo[:1].astype(BF16), 0)]

    x = x.reshape(n, d)
    for l in range(depth):
        x, converted = _ffn(x, (ffn1_g[0], l), *ffn1_w, (ffn1_g[1], l),
                            casts=[(w, l) for w in (*mixer_weights, ffn2_wi, ffn2_wo)])
        (w_in_l, w_pa_l, w_pb_l, w_pc_l, w_o_l), ffn2_w = converted[:5], converted[5:]
        *qkvbs, qkvc = _proj(x, (mix_pre_g, l), w_in_l)
        b_parts = [_band_attention(qkvb, g, bsz) for g, qkvb in enumerate(qkvbs)]
        yc = _stick_breaking(qkvc.reshape(bsz, s, 3 * C_WIDTH)).reshape(n, C_WIDTH)
        x = _merge(x, b_parts, yc, (mix_pre_g, l), w_in_l, (a_ln_g, l), (a_ln_b, l), (a_ws, l), (bs_full, l),
                   w_pa_l, w_pb_l, w_pc_l, w_o_l, (mix_post_g, l))
        next_ffn1 = [(w, l + 1) for w in (ffn1_wi, ffn1_wo)] if l + 1 < depth else []
        x, ffn1_w = _ffn(x, (ffn2_g[0], l), *ffn2_w, (ffn2_g[1], l), casts=next_ffn1)
    return x.reshape(bsz, s, d)
```

```python
import functools

import jax
import jax.numpy as jnp
from jax import lax
from jax.experimental import pallas as pl
from jax.experimental.pallas import tpu as pltpu

D_MODEL = 1024
HEAD_DIM = 64
A_GROUPS = 4
A_CHUNK = 128
A_GROUP_DIM = 128
A_WIDTH = A_GROUPS * A_GROUP_DIM
B_PATTERNS = ((128, 1), (512, 4), (2048, 16))
B_GROUPS = len(B_PATTERNS)
B_HEADS_PER_GROUP = 4
B_GROUP_WIDTH = B_HEADS_PER_GROUP * HEAD_DIM
B_WIDTH = B_GROUPS * B_GROUP_WIDTH
B_BLOCK = 128
B_TASKS = 2
C_HEADS = 8
C_WIDTH = C_HEADS * HEAD_DIM
C_BLOCK = 256
C_HEADS_PER_STEP = 8
SPLIT_A = 2 * A_WIDTH
SPLIT_B = SPLIT_A + 3 * B_WIDTH
SPLIT_C = SPLIT_B + 3 * C_WIDTH
D_FF = 2816
EPS = 1e-6
SCALE = HEAD_DIM ** -0.5
LOG2E = 1.4426950408889634
B_Q_SCALE = SCALE * LOG2E
C_Q_SCALE = -SCALE * LOG2E
C_DEAD_LOG2 = -160.0

F32 = jnp.float32
BF16 = jnp.bfloat16

LANES = 128

VMEM_LIMIT_BYTES = 60 * 1024 * 1024

FFN_ROWS = 1024
FFN_COLS = 256
FFN_SUBTILES = 4
PROJ_ROWS = 1024
MERGE_ROWS = 512
MERGE_SUBTILES = 2


def _params(*semantics):
    return pltpu.CompilerParams(dimension_semantics=semantics, vmem_limit_bytes=VMEM_LIMIT_BYTES)


def _rms(x, g):
    return x * lax.rsqrt(jnp.mean(x * x, axis=-1, keepdims=True) + EPS) * g


def _sigmoid(x):
    return 0.5 * jnp.tanh(0.5 * x) + 0.5


def _resident(param):
    stacked, layer = param
    zeros = (0,) * (stacked.ndim - 1)
    return pl.BlockSpec((None, *stacked.shape[1:]), lambda *_: (layer, *zeros), pipeline_mode=pl.Buffered(1))


def _ffn_kernel(x_ref, pre_g_ref, wi_ref, wo_ref, post_g_ref, *refs):
    n_casts = (len(refs) - 2) // 2
    o_ref, act_ref = refs[n_casts], refs[-1]
    for src_ref, dst_ref in zip(refs[:n_casts], refs[n_casts + 1:-1]):
        dst_ref[...] = src_ref[...].astype(BF16)

    sub = FFN_ROWS // FFN_SUBTILES
    ranges = [slice(i * sub, (i + 1) * sub) for i in range(FFN_SUBTILES)]
    hs = [_rms(x_ref[rows, :], pre_g_ref[...]).astype(BF16) for rows in ranges]
    for rows, h in zip(ranges, hs):
        for c in range(D_FF // FFN_COLS):
            gate = jnp.dot(h, wi_ref[:, c * FFN_COLS:(c + 1) * FFN_COLS], preferred_element_type=F32)
            up = jnp.dot(h, wi_ref[:, D_FF + c * FFN_COLS:D_FF + (c + 1) * FFN_COLS], preferred_element_type=F32)
            act_ref[rows, c * FFN_COLS:(c + 1) * FFN_COLS] = (gate * jax.nn.sigmoid(gate) * up).astype(BF16)
    ys = [jnp.dot(act_ref[rows, :], wo_ref[...], preferred_element_type=F32) for rows in ranges]
    for rows, y in zip(ranges, ys):
        o_ref[rows, :] = x_ref[rows, :] + 0.5 * _rms(y, post_g_ref[...])


def _ffn(x, pre_g, wi, wo, post_g, casts=()):
    n = x.shape[0]
    steps = n // FFN_ROWS
    rows = pl.BlockSpec((FFN_ROWS, D_MODEL), lambda i: (i, 0))
    params = (pre_g, wi, wo, post_g)
    cast_in, cast_out, cast_shapes = [], [], []
    for stacked, layer in casts:
        r, c = stacked.shape[1:]
        block = (None, r // steps, c)
        cast_in.append(pl.BlockSpec(block, lambda i, layer=layer: (layer, i, 0)))
        cast_out.append(pl.BlockSpec(block, lambda i: (0, i, 0)))
        cast_shapes.append(jax.ShapeDtypeStruct((1, r, c), BF16))
    out, *converted = pl.pallas_call(
        _ffn_kernel,
        grid=(steps,),
        in_specs=[rows] + [_resident(p) for p in params] + cast_in,
        out_specs=[rows] + cast_out,
        out_shape=[jax.ShapeDtypeStruct((n, D_MODEL), F32)] + cast_shapes,
        scratch_shapes=[pltpu.VMEM((FFN_ROWS, D_FF), BF16)],
        compiler_params=_params("parallel"),
        name="ffn",
    )(x, *(p[0] for p in params), *(p[0] for p in casts))
    return out, [(w, 0) for w in converted]


def _proj_kernel(x_ref, g_ref, w_in_ref, qkvb0_ref, qkvb1_ref, qkvb2_ref, qkvc_ref, class_ref):
    h = _rms(x_ref[...], g_ref[...]).astype(BF16)
    tiles = B_GROUP_WIDTH // LANES
    for g, (out_ref, (_, dil)) in enumerate(zip((qkvb0_ref, qkvb1_ref, qkvb2_ref), B_PATTERNS)):
        for part in range(3):
            first = SPLIT_A + part * B_WIDTH + g * B_GROUP_WIDTH
            y = jnp.dot(h, w_in_ref[:, first:first + B_GROUP_WIDTH], preferred_element_type=F32)
            if part == 0:
                y = y * B_Q_SCALE
            if dil == 1:
                out_ref[:, part * B_GROUP_WIDTH:(part + 1) * B_GROUP_WIDTH] = y.astype(BF16)
            else:
                for c in range(tiles):
                    class_ref[part * tiles + c] = y[:, c * LANES:(c + 1) * LANES]
        if dil > 1:
            width = 3 * B_GROUP_WIDTH
            for r in range(dil):
                for c in range(width // LANES):
                    lanes = slice(r * width + c * LANES, r * width + (c + 1) * LANES)
                    out_ref[:, lanes] = class_ref[c, pl.ds(r, PROJ_ROWS // dil, stride=dil), :].astype(BF16)
    qkvc = jnp.dot(h, w_in_ref[:, SPLIT_B:SPLIT_C], preferred_element_type=F32)
    qkvc_ref[:, :C_WIDTH] = (qkvc[:, :C_WIDTH] * C_Q_SCALE).astype(BF16)
    qkvc_ref[:, C_WIDTH:] = qkvc[:, C_WIDTH:].astype(BF16)


def _proj(x, g, w_in):
    n = x.shape[0]
    rows = lambda w, dil=1: pl.BlockSpec((PROJ_ROWS // dil, dil * w), lambda i: (i, 0))
    widths = [3 * B_GROUP_WIDTH] * B_GROUPS + [3 * C_WIDTH]
    dils = [*(dil for _, dil in B_PATTERNS), 1]
    return pl.pallas_call(
        _proj_kernel,
        grid=(n // PROJ_ROWS,),
        in_specs=[rows(D_MODEL), _resident(g), _resident(w_in)],
        out_specs=[rows(w, dil) for w, dil in zip(widths, dils)],
        out_shape=[jax.ShapeDtypeStruct((n // dil, dil * w), BF16) for w, dil in zip(widths, dils)],
        scratch_shapes=[pltpu.VMEM((3 * B_GROUP_WIDTH // LANES, PROJ_ROWS, LANES), F32)],
        compiler_params=_params("parallel"),
        name="proj",
    )(x, g[0], w_in[0])


def _band_attention_kernel(qkv_ref, o_ref, lse_ref, *, dil):
    length = qkv_ref.shape[0]
    n_blocks = length // B_BLOCK
    window = min(2 * B_BLOCK, length)
    r_idx = lax.broadcasted_iota(jnp.int32, (B_BLOCK, window), 0)
    c_idx = lax.broadcasted_iota(jnp.int32, (B_BLOCK, window), 1)
    first_head = lax.broadcasted_iota(jnp.int32, (B_BLOCK, 2 * HEAD_DIM), 1) < HEAD_DIM
    contract_last = (((1,), (1,)), ((), ()))
    pair_width = 2 * HEAD_DIM
    n_pairs = B_GROUP_WIDTH // pair_width

    def blocks(tasks):
        qs, ks, vs, valids, dests = [], [], [], [], []
        for r, i in tasks:
            base = r * 3 * B_GROUP_WIDTH
            cur = pl.ds(pl.multiple_of(i * B_BLOCK, B_BLOCK), B_BLOCK)
            start = jnp.maximum(i - 1, 0) * B_BLOCK if n_blocks > 1 else 0
            keys = pl.ds(pl.multiple_of(start, B_BLOCK), window)
            ahead = r_idx + (i * B_BLOCK - start) - c_idx
            valid = jnp.logical_and(ahead >= 0, ahead <= B_BLOCK)
            for p in range(n_pairs):
                lanes = lambda part: slice(base + part * B_GROUP_WIDTH + p * pair_width,
                                           base + part * B_GROUP_WIDTH + (p + 1) * pair_width)
                q2 = qkv_ref[cur, lanes(0)]
                zero = jnp.zeros_like(q2)
                qs += [jnp.where(first_head, q2, zero), jnp.where(first_head, zero, q2)]
                ks += [qkv_ref[keys, lanes(1)]] * 2
                vs += [qkv_ref[keys, lanes(2)]] * 2
                valids += [valid] * 2
                dests.append((cur, slice(r * B_GROUP_WIDTH + p * pair_width,
                                         r * B_GROUP_WIDTH + (p + 1) * pair_width)))
        scores = [lax.dot_general(q, k, contract_last, preferred_element_type=F32) for q, k in zip(qs, ks)]
        probs, stats = [], []
        for s, valid in zip(scores, valids):
            s = jnp.where(valid, s, -jnp.inf)
            m = jnp.max(s, axis=1, keepdims=True)
            e = jnp.exp2(s - m)
            l = jnp.sum(e, axis=1, keepdims=True)
            probs.append(e.astype(BF16))
            stats.append((m, l))
        outs = [jnp.dot(p, v, preferred_element_type=F32) for p, v in zip(probs, vs)]
        for d, (rows, lanes) in enumerate(dests):
            (m0, l0), (m1, l1) = stats[2 * d], stats[2 * d + 1]
            l = jnp.where(first_head, l0, l1)
            o_ref[rows, lanes] = jnp.where(first_head, outs[2 * d], outs[2 * d + 1]) / l
            lse_ref[rows, lanes] = jnp.where(first_head, m0, m1) + jnp.log2(l)

    if n_blocks == 1:
        for r in range(0, dil, B_TASKS):
            blocks([(r + t, 0) for t in range(B_TASKS)])
    else:
        for r in range(dil):
            def body(step, carry, r=r):
                blocks([(r, step * B_TASKS + t) for t in range(B_TASKS)])
                return carry
            lax.fori_loop(0, n_blocks // B_TASKS, body, 0)


def _band_attention_groups_kernel(*refs):
    for g, (_, dil) in enumerate(B_PATTERNS):
        _band_attention_kernel(refs[g], refs[B_GROUPS + 2 * g], refs[B_GROUPS + 2 * g + 1], dil=dil)


def _band_attention(qkvbs, bsz):
    views, in_specs, out_specs, out_shapes = [], [], [], []
    for qkvb, (_, dil) in zip(qkvbs, B_PATTERNS):
        length = qkvb.shape[0] // bsz
        views.append(qkvb.reshape(bsz, length, qkvb.shape[1]))
        in_specs.append(pl.BlockSpec((None, length, qkvb.shape[1]), lambda b: (b, 0, 0)))
        out_specs += [pl.BlockSpec((None, length, dil * B_GROUP_WIDTH), lambda b: (b, 0, 0))] * 2
        out_shapes += [jax.ShapeDtypeStruct((bsz, length, dil * B_GROUP_WIDTH), F32)] * 2
    outs = pl.pallas_call(
        _band_attention_groups_kernel,
        grid=(bsz,),
        in_specs=in_specs,
        out_specs=out_specs,
        out_shape=out_shapes,
        compiler_params=_params("parallel"),
        name="band_attention",
    )(*views)
    flat = [a.reshape(-1, a.shape[-1]) for a in outs]
    return [(flat[2 * g], flat[2 * g + 1]) for g in range(B_GROUPS)]


def _stick_breaking_tiles(qs, ks, vs, upper, mask):
    contract_last = (((1,), (1,)), ((), ()))
    zs = [lax.dot_general(q, k, contract_last, preferred_element_type=F32) for q, k in zip(qs, ks)]
    log_keeps, logits, afters = [], [], []
    for nw in zs:
        log_keep = jnp.minimum(nw, 0.0) - jnp.log2(1.0 + jnp.exp2(-jnp.abs(nw)))
        if mask is not None:
            log_keep = jnp.where(mask, log_keep, 0.0)
        afters.append(jnp.dot(log_keep.astype(BF16), upper, preferred_element_type=F32))
        log_keeps.append(log_keep)
        logits.append(log_keep - nw)
    out = []
    for log_keep, logit, after, v in zip(log_keeps, logits, afters, vs):
        a = jnp.exp2(logit + after)
        if mask is not None:
            a = jnp.where(mask, a, 0.0)
        contrib = jnp.dot(a.astype(BF16), v, preferred_element_type=F32)
        out.append((contrib, jnp.sum(log_keep, axis=1, keepdims=True)))
    return out


def _stick_breaking_kernel(q_ref, k_ref, v_ref, o_ref, acc_ref, passed_ref):
    n_blocks = q_ref.shape[0] // C_BLOCK
    row = lax.broadcasted_iota(jnp.int32, (C_BLOCK, C_BLOCK), 0)
    col = lax.broadcasted_iota(jnp.int32, (C_BLOCK, C_BLOCK), 1)
    strictly_lower = col < row
    upper = jnp.where(row > col, 1.0, 0.0).astype(BF16)
    pair_width = 2 * HEAD_DIM
    pairs = [slice(p * pair_width, (p + 1) * pair_width) for p in range(C_HEADS_PER_STEP // 2)]
    first_head = lax.broadcasted_iota(jnp.int32, (C_BLOCK, pair_width), 1) < HEAD_DIM

    def q_block(i, carry):
        rows_i = pl.ds(pl.multiple_of(i * C_BLOCK, C_BLOCK), C_BLOCK)

        def tiles(rows_j, mask):
            qs, ks, vs = [], [], []
            for lanes in pairs:
                q2 = q_ref[rows_i, lanes]
                zero = jnp.zeros_like(q2)
                qs += [jnp.where(first_head, q2, zero), jnp.where(first_head, zero, q2)]
                ks += [k_ref[rows_j, lanes]] * 2
                vs += [v_ref[rows_j, lanes]] * 2
            res = _stick_breaking_tiles(qs, ks, vs, upper, mask)
            return [(jnp.where(first_head, res[2 * p][0], res[2 * p + 1][0]),
                     jnp.where(first_head, res[2 * p][1], res[2 * p + 1][1])) for p in range(len(pairs))]

        for p, (contrib, row_sum) in enumerate(tiles(rows_i, strictly_lower)):
            acc_ref[p] = contrib
            passed_ref[p] = row_sum

        def k_block(state):
            step, _ = state
            rows_j = pl.ds(pl.multiple_of((i - 1 - step) * C_BLOCK, C_BLOCK), C_BLOCK)
            most_alive = None
            for p, (contrib, row_sum) in enumerate(tiles(rows_j, None)):
                passed = passed_ref[p]
                acc_ref[p] += contrib * jnp.exp2(passed)
                passed = passed + row_sum
                passed_ref[p] = passed
                most_alive = passed if most_alive is None else jnp.maximum(most_alive, passed)
            return step + 1, jnp.max(most_alive)

        def more_to_do(state):
            step, most_alive = state
            return jnp.logical_and(step < i, most_alive > C_DEAD_LOG2)

        lax.while_loop(more_to_do, k_block, (jnp.int32(0), jnp.float32(0.0)))
        for p, lanes in enumerate(pairs):
            o_ref[rows_i, lanes] = acc_ref[p].astype(o_ref.dtype)
        return carry

    lax.fori_loop(0, n_blocks, q_block, 0)


def _stick_breaking(qkvc):
    bsz, s, _ = qkvc.shape
    width = C_HEADS_PER_STEP * HEAD_DIM
    n_steps = C_WIDTH // width

    def spec(part):
        return pl.BlockSpec((None, s, width), lambda b, p: (b, 0, part * n_steps + p))

    return pl.pallas_call(
        _stick_breaking_kernel,
        grid=(bsz, n_steps),
        in_specs=[spec(0), spec(1), spec(2)],
        out_specs=pl.BlockSpec((None, s, width), lambda b, p: (b, 0, p)),
        out_shape=jax.ShapeDtypeStruct((bsz, s, C_WIDTH), BF16),
        scratch_shapes=[pltpu.VMEM((C_HEADS_PER_STEP // 2, C_BLOCK, 2 * HEAD_DIM), F32),
                        pltpu.VMEM((C_HEADS_PER_STEP // 2, C_BLOCK, 2 * HEAD_DIM), F32)],
        compiler_params=_params("parallel", "parallel"),
        name="stick_breaking",
    )(qkvc, qkvc, qkvc)


def _fill_token_order(src_ref, scratch_ref, dil):
    tiles, rows, _ = scratch_ref.shape
    for r in range(dil):
        for c in range(tiles):
            lanes = slice((r * tiles + c) * LANES, (r * tiles + c + 1) * LANES)
            scratch_ref[c, pl.ds(r, rows // dil, stride=dil), :] = src_ref[:, lanes]


def _merge_kernel(x_ref, o0_ref, l0_ref, o1_ref, l1_ref, o2_ref, l2_ref, yc_ref,
                  pre_g_ref, w_in_ref, ln_g_ref, ln_b_ref, ws_ref, bs_ref,
                  wpa_ref, wpb_ref, wpc_ref, wo_ref, post_g_ref, out_ref, *order_refs):
    sub = MERGE_ROWS // MERGE_SUBTILES
    ranges = [slice(i * sub, (i + 1) * sub) for i in range(MERGE_SUBTILES)]
    dot = functools.partial(jnp.dot, preferred_element_type=F32)

    b_srcs = (o0_ref, l0_ref, o1_ref, l1_ref, o2_ref, l2_ref)
    b_dils = [dil for _, dil in B_PATTERNS for _ in range(2)]
    for src, scratch, dil in zip(b_srcs, order_refs, b_dils):
        if dil > 1:
            _fill_token_order(src, scratch, dil)

    def b_part(k, rows):
        if b_dils[k] == 1:
            return b_srcs[k][rows, :]
        return jnp.concatenate([order_refs[k][c, rows, :] for c in range(order_refs[k].shape[0])], axis=1)

    hs = [_rms(x_ref[rows, :], pre_g_ref[...]).astype(BF16) for rows in ranges]
    gate = lambda h, b: dot(h, w_in_ref[:, SPLIT_C + b * D_MODEL:SPLIT_C + (b + 1) * D_MODEL])
    zas = [dot(h, w_in_ref[:, :SPLIT_A]) for h in hs]

    p_cs = [dot(yc_ref[rows, :], wpc_ref[...]) for rows in ranges]
    mergeds = [_sigmoid(gate(h, 2)) * p_c for h, p_c in zip(hs, p_cs)]

    y_bs = []
    for rows in ranges:
        o0, l0, o1, l1, o2, l2 = [b_part(k, rows) for k in range(6)]
        m = jnp.maximum(jnp.maximum(l0, l1), l2)
        w0, w1, w2 = jnp.exp2(l0 - m), jnp.exp2(l1 - m), jnp.exp2(l2 - m)
        y_bs.append(((w0 * o0 + w1 * o1 + w2 * o2) / (w0 + w1 + w2)).astype(BF16))
    p_bs = [dot(y_b, wpb_ref[...]) for y_b in y_bs]
    mergeds = [merged + _sigmoid(gate(h, 1)) * p_b for merged, h, p_b in zip(mergeds, hs, p_bs)]

    us, vs = [], []
    for za in zas:
        z = jax.nn.gelu(za, approximate=True)
        v = z[:, A_WIDTH:]
        vc = v - jnp.mean(v, axis=-1, keepdims=True)
        v = vc * lax.rsqrt(jnp.mean(vc * vc, axis=-1, keepdims=True) + EPS) * ln_g_ref[...] + ln_b_ref[...]
        us.append(z[:, :A_WIDTH])
        vs.append(v.astype(BF16))
    t_idx = lax.broadcasted_iota(jnp.int32, (A_CHUNK, A_CHUNK), 0)
    s_idx = lax.broadcasted_iota(jnp.int32, (A_CHUNK, A_CHUNK), 1)
    ws = [jnp.where(s_idx <= t_idx, ws_ref[g], 0.0).astype(BF16) for g in range(A_GROUPS)]
    n_chunks = sub // A_CHUNK
    y_as = []
    for u, v in zip(us, vs):
        mixed = []
        for g in range(A_GROUPS):
            vg = jnp.concatenate([v[c * A_CHUNK:(c + 1) * A_CHUNK, g * A_GROUP_DIM:(g + 1) * A_GROUP_DIM]
                                  for c in range(n_chunks)], axis=1)
            mixed.append(dot(ws[g], vg))
        sv = jnp.concatenate(
            [jnp.concatenate([m[:, c * A_GROUP_DIM:(c + 1) * A_GROUP_DIM] for m in mixed], axis=1) + bs_ref[...]
             for c in range(n_chunks)], axis=0)
        y_as.append((u * sv).astype(BF16))
    p_as = [dot(y_a, wpa_ref[...]) for y_a in y_as]
    mergeds = [merged + _sigmoid(gate(h, 0)) * p_a for merged, h, p_a in zip(mergeds, hs, p_as)]

    ys = [dot(merged.astype(BF16), wo_ref[...]) for merged in mergeds]
    for rows, y in zip(ranges, ys):
        out_ref[rows, :] = x_ref[rows, :] + _rms(y, post_g_ref[...])


def _merge(x, b_parts, yc, *params):
    n = x.shape[0]
    rows = lambda w: pl.BlockSpec((MERGE_ROWS, w), lambda i: (i, 0))
    b_flat = [a for part in b_parts for a in part]
    b_specs = [pl.BlockSpec((MERGE_ROWS // dil, dil * B_GROUP_WIDTH), lambda i: (i, 0))
               for _, dil in B_PATTERNS for _ in range(2)]
    return pl.pallas_call(
        _merge_kernel,
        grid=(n // MERGE_ROWS,),
        in_specs=[rows(D_MODEL)] + b_specs + [rows(C_WIDTH)] + [_resident(p) for p in params],
        out_specs=rows(D_MODEL),
        out_shape=jax.ShapeDtypeStruct((n, D_MODEL), F32),
        scratch_shapes=[pltpu.VMEM((B_GROUP_WIDTH // LANES, MERGE_ROWS, LANES), F32)] * (2 * B_GROUPS),
        compiler_params=_params("parallel"),
        name="merge",
    )(x, *b_flat, yc, *(p[0] for p in params))


def kernel(x, ffn1_pre_g, ffn1_wi, ffn1_wo, ffn1_post_g, mix_pre_g, w_in, a_ln_g, a_ln_b, a_ws, a_bs,
           w_pa, w_pb, w_pc, w_o, mix_post_g, ffn2_pre_g, ffn2_wi, ffn2_wo, ffn2_post_g):
    bsz, s, d = x.shape
    n = bsz * s
    depth = ffn1_wi.shape[0]
    rows = lambda p: p[:, None, :]
    ffn1_g, ffn2_g = (rows(ffn1_pre_g), rows(ffn1_post_g)), (rows(ffn2_pre_g), rows(ffn2_post_g))
    mix_pre_g, mix_post_g, a_ln_g, a_ln_b = rows(mix_pre_g), rows(mix_post_g), rows(a_ln_g), rows(a_ln_b)
    bs_full = jnp.repeat(jnp.swapaxes(a_bs, 1, 2), A_GROUP_DIM, axis=2)
    mixer_weights = (w_in, w_pa, w_pb, w_pc, w_o)
    ffn1_w = [(ffn1_wi[:1].astype(BF16), 0), (ffn1_wo[:1].astype(BF16), 0)]

    x = x.reshape(n, d)
    for l in range(depth):
        x, converted = _ffn(x, (ffn1_g[0], l), *ffn1_w, (ffn1_g[1], l),
                            casts=[(w, l) for w in (*mixer_weights, ffn2_wi, ffn2_wo)])
        (w_in_l, w_pa_l, w_pb_l, w_pc_l, w_o_l), ffn2_w = converted[:5], converted[5:]
        *qkvbs, qkvc = _proj(x, (mix_pre_g, l), w_in_l)
        b_parts = _band_attention(qkvbs, bsz)
        yc = _stick_breaking(qkvc.reshape(bsz, s, 3 * C_WIDTH)).reshape(n, C_WIDTH)
        x = _merge(x, b_parts, yc, (mix_pre_g, l), w_in_l, (a_ln_g, l), (a_ln_b, l), (a_ws, l), (bs_full, l),
                   w_pa_l, w_pb_l, w_pc_l, w_o_l, (mix_post_g, l))
        next_ffn1 = [(w, l + 1) for w in (ffn1_wi, ffn1_wo)] if l + 1 < depth else []
        x, ffn1_w = _ffn(x, (ffn2_g[0], l), *ffn2_w, (ffn2_g[1], l), casts=next_ffn1)
    return x.reshape(bsz, s, d)
```
